```python
import jax, jax.numpy as jnp
from jax import lax
import numpy as np

D_MODEL = 1024
BATCH = 2
SEQ = 8192
DEPTH = 1
DEC_BATCH = 4
DEC_SEQ = 4096
PAST_LEN = 128

N_MEM = 256
MLA_HEADS = 8
QK_NOPE = 64
QK_ROPE = 32
QK_HEAD = QK_NOPE + QK_ROPE
V_HEAD = 64
Q_LORA = 384
KV_LORA = 256
CONV_WIDTH = 512
CONV_K = 3
XA_HEADS = 4
XA_HEAD = 128
N_BRANCH = 3
D_FF = 2816
ROPE_BASE = 10000.0
EPS = 1e-6
Q_BLOCK = 128

IN_SPLITS = (Q_LORA, KV_LORA, QK_ROPE, CONV_WIDTH, CONV_WIDTH, CONV_WIDTH, XA_HEADS * XA_HEAD, N_BRANCH * D_MODEL)
D_IN = Q_LORA + KV_LORA + QK_ROPE + 3 * CONV_WIDTH + XA_HEADS * XA_HEAD + N_BRANCH * D_MODEL

kernel_name = "hybrid_mla_shortconv_memory_encoder"


def _split_points():
    pts, acc = [], 0
    for w in IN_SPLITS[:-1]:
        acc += w
        pts.append(acc)
    return pts


def rmsnorm(x, g):
    xf = x.astype(jnp.float32)
    inv = lax.rsqrt(jnp.mean(xf * xf, axis=-1, keepdims=True) + EPS)
    return (xf * inv).astype(x.dtype) * g


def rope(x, pos):
    half = QK_ROPE // 2
    inv_freq = ROPE_BASE ** (-jnp.arange(half, dtype=jnp.float32) / half)
    ang = pos.astype(jnp.float32)[:, None] * inv_freq[None, :]
    cos = jnp.cos(ang)[:, None, :]
    sin = jnp.sin(ang)[:, None, :]
    xf = x.astype(jnp.float32)
    x1, x2 = xf[..., :half], xf[..., half:]
    return jnp.concatenate([x1 * cos - x2 * sin, x2 * cos + x1 * sin], axis=-1).astype(x.dtype)


def swiglu(x, w_gu, w_down):
    g, u = jnp.split(x @ w_gu, 2, axis=-1)
    return (jax.nn.silu(g) * u) @ w_down


def mla_attention(q, k, v):
    B, S, H, _ = q.shape
    nb = S // Q_BLOCK
    scale = QK_HEAD ** -0.5
    qb = q.reshape(B, nb, Q_BLOCK, H, QK_HEAD).transpose(1, 0, 2, 3, 4)

    def one_block(qblk):
        s = jnp.einsum('bqhd,bkhd->bhqk', qblk, k, preferred_element_type=jnp.float32) * scale
        p = jax.nn.softmax(s, axis=-1)
        return jnp.einsum('bhqk,bkhd->bqhd', p.astype(v.dtype), v)

    o = lax.map(one_block, qb)
    return o.transpose(1, 0, 2, 3, 4).reshape(B, S, H * V_HEAD)


def short_conv(u, w):
    S = u.shape[1]
    pad = CONV_K // 2
    up = jnp.pad(u, ((0, 0), (pad, pad), (0, 0)))
    y = up[:, 0:S] * w[0]
    for j in range(1, CONV_K):
        y = y + up[:, j:j + S] * w[j]
    return y


def cross_attention(q, k, v):
    B, S = q.shape[0], q.shape[1]
    s = jnp.einsum('bqhd,bmhd->bhqm', q, k, preferred_element_type=jnp.float32) * (XA_HEAD ** -0.5)
    p = jax.nn.softmax(s, axis=-1)
    o = jnp.einsum('bhqm,bmhd->bqhd', p.astype(v.dtype), v)
    return o.reshape(B, S, XA_HEADS * XA_HEAD)


def trunk(x, mem, ffn1_norm, ffn1_w_gu, ffn1_w_down, mix_norm, w_in, q_lora_norm, w_uq,
          kv_lora_norm, w_uk, w_uv, mla_q_norm, mla_k_norm, w_o_mla, conv_w, w_o_conv,
          mem_norm, w_mem_kv, xa_q_norm, xa_k_norm, w_o_mem, w_out, ffn2_norm, ffn2_w_gu, ffn2_w_down):
    B, S, _ = x.shape
    pos = jnp.arange(S, dtype=jnp.int32)
    splits = _split_points()
    for l in range(DEPTH):
        x = x + 0.5 * swiglu(rmsnorm(x, ffn1_norm[l]), ffn1_w_gu[l], ffn1_w_down[l])

        h = rmsnorm(x, mix_norm[l])
        c_q, c_kv, k_r, cb, cc, cx, xq, glog = jnp.split(h @ w_in[l], splits, axis=-1)

        q = (rmsnorm(c_q, q_lora_norm[l]) @ w_uq[l]).reshape(B, S, MLA_HEADS, QK_HEAD)
        c_kv = rmsnorm(c_kv, kv_lora_norm[l])
        k_nope = (c_kv @ w_uk[l]).reshape(B, S, MLA_HEADS, QK_NOPE)
        v = (c_kv @ w_uv[l]).reshape(B, S, MLA_HEADS, V_HEAD)
        k = jnp.concatenate([k_nope, jnp.broadcast_to(k_r[:, :, None, :], (B, S, MLA_HEADS, QK_ROPE))], axis=-1)
        q = rmsnorm(q, mla_q_norm[l])
        k = rmsnorm(k, mla_k_norm[l])
        q = jnp.concatenate([q[..., :QK_NOPE], rope(q[..., QK_NOPE:], pos)], axis=-1)
        k = jnp.concatenate([k[..., :QK_NOPE], rope(k[..., QK_NOPE:], pos)], axis=-1)
        y_mla = mla_attention(q, k, v) @ w_o_mla[l]

        y_conv = (cb * short_conv(cc * cx, conv_w[l])) @ w_o_conv[l]

        m = rmsnorm(mem, mem_norm[l])
        mk, mv = jnp.split(m @ w_mem_kv[l], 2, axis=-1)
        Bm, M = mem.shape[0], mem.shape[1]
        mk = rmsnorm(mk.reshape(Bm, M, XA_HEADS, XA_HEAD), xa_k_norm[l])
        mv = mv.reshape(Bm, M, XA_HEADS, XA_HEAD)
        xq = rmsnorm(xq.reshape(B, S, XA_HEADS, XA_HEAD), xa_q_norm[l])
        y_mem = cross_attention(xq, mk, mv) @ w_o_mem[l]

        gates = jax.nn.sigmoid(glog.reshape(B, S, N_BRANCH, D_MODEL))
        merged = gates[:, :, 0] * y_mla + gates[:, :, 1] * y_conv + gates[:, :, 2] * y_mem
        x = x + merged @ w_out[l]

        x = x + 0.5 * swiglu(rmsnorm(x, ffn2_norm[l]), ffn2_w_gu[l], ffn2_w_down[l])
    return x


def setup_inputs(seed: int = 0) -> dict:
    key = jax.random.key(seed)
    ks = jax.random.split(key, 32)

    def dense(k, shape, fan_in):
        return jax.random.normal(k, shape, jnp.float32) * (fan_in ** -0.5)

    def gain(k, n):
        return 1.0 + 0.02 * jax.random.normal(k, (DEPTH, n), jnp.float32)

    L = DEPTH
    return {
        "x_prompt": jax.random.normal(ks[0], (BATCH, SEQ, D_MODEL), jnp.float32),
        "x_sample": jax.random.normal(ks[1], (DEC_BATCH, DEC_SEQ, D_MODEL), jnp.float32),
        "mem_prompt": jax.random.normal(ks[2], (BATCH, N_MEM, D_MODEL), jnp.float32),
        "mem_sample": jax.random.normal(ks[3], (DEC_BATCH, N_MEM, D_MODEL), jnp.float32),
        "ffn1_norm": gain(ks[4], D_MODEL),
        "ffn1_w_gu": dense(ks[5], (L, D_MODEL, 2 * D_FF), D_MODEL),
        "ffn1_w_down": dense(ks[6], (L, D_FF, D_MODEL), D_FF),
        "mix_norm": gain(ks[7], D_MODEL),
        "w_in": dense(ks[8], (L, D_MODEL, D_IN), D_MODEL),
        "q_lora_norm": gain(ks[9], Q_LORA),
        "w_uq": dense(ks[10], (L, Q_LORA, MLA_HEADS * QK_HEAD), Q_LORA),
        "kv_lora_norm": gain(ks[11], KV_LORA),
        "w_uk": dense(ks[12], (L, KV_LORA, MLA_HEADS * QK_NOPE), KV_LORA),
        "w_uv": dense(ks[13], (L, KV_LORA, MLA_HEADS * V_HEAD), KV_LORA),
        "mla_q_norm": gain(ks[14], QK_HEAD),
        "mla_k_norm": gain(ks[15], QK_HEAD),
        "w_o_mla": dense(ks[16], (L, MLA_HEADS * V_HEAD, D_MODEL), MLA_HEADS * V_HEAD),
        "conv_w": dense(ks[17], (L, CONV_K, CONV_WIDTH), CONV_K),
        "w_o_conv": dense(ks[18], (L, CONV_WIDTH, D_MODEL), CONV_WIDTH),
        "mem_norm": gain(ks[19], D_MODEL),
        "w_mem_kv": dense(ks[20], (L, D_MODEL, 2 * XA_HEADS * XA_HEAD), D_MODEL),
        "xa_q_norm": gain(ks[21], XA_HEAD),
        "xa_k_norm": gain(ks[22], XA_HEAD),
        "w_o_mem": dense(ks[23], (L, XA_HEADS * XA_HEAD, D_MODEL), XA_HEADS * XA_HEAD),
        "w_out": dense(ks[24], (L, D_MODEL, D_MODEL), D_MODEL),
        "ffn2_norm": gain(ks[25], D_MODEL),
        "ffn2_w_gu": dense(ks[26], (L, D_MODEL, 2 * D_FF), D_MODEL),
        "ffn2_w_down": dense(ks[27], (L, D_FF, D_MODEL), D_FF),
    }


def reference(x_prompt, x_sample, mem_prompt, mem_sample, ffn1_norm, ffn1_w_gu, ffn1_w_down,
              mix_norm, w_in, q_lora_norm, w_uq, kv_lora_norm, w_uk, w_uv, mla_q_norm, mla_k_norm,
              w_o_mla, conv_w, w_o_conv, mem_norm, w_mem_kv, xa_q_norm, xa_k_norm, w_o_mem, w_out,
              ffn2_norm, ffn2_w_gu, ffn2_w_down):
    y_prompt = trunk(x_prompt, mem_prompt, ffn1_norm, ffn1_w_gu, ffn1_w_down, mix_norm, w_in,
                     q_lora_norm, w_uq, kv_lora_norm, w_uk, w_uv, mla_q_norm, mla_k_norm, w_o_mla,
                     conv_w, w_o_conv, mem_norm, w_mem_kv, xa_q_norm, xa_k_norm, w_o_mem, w_out,
                     ffn2_norm, ffn2_w_gu, ffn2_w_down)
    y_sample = trunk(x_sample, mem_sample, ffn1_norm, ffn1_w_gu, ffn1_w_down, mix_norm, w_in,
                     q_lora_norm, w_uq, kv_lora_norm, w_uk, w_uv, mla_q_norm, mla_k_norm, w_o_mla,
                     conv_w, w_o_conv, mem_norm, w_mem_kv, xa_q_norm, xa_k_norm, w_o_mem, w_out,
                     ffn2_norm, ffn2_w_gu, ffn2_w_down)
    return (y_prompt, y_sample)
```

```python
import functools

import jax
import jax.numpy as jnp
from jax import lax
from jax.experimental import pallas as pl
from jax.experimental.pallas import tpu as pltpu

D_MODEL = 1024
N_MEM = 256
MLA_HEADS = 8
QK_NOPE = 64
QK_ROPE = 32
QK_HEAD = QK_NOPE + QK_ROPE
V_HEAD = 64
Q_LORA = 384
KV_LORA = 256
CONV_WIDTH = 512
XA_HEADS = 4
XA_HEAD = 128
D_FF = 2816
ROPE_BASE = 10000.0
EPS = 1e-6

LANES = 128
HEAD_PAD = LANES
FF_CHUNK = 256
N_FF_CHUNKS = D_FF // FF_CHUNK
VMEM_LIMIT = 56 * 1024 * 1024

BF16 = jnp.bfloat16
F32 = jnp.float32


def _const_spec(shape):
    zeros = (0,) * len(shape)
    return pl.BlockSpec(shape, lambda *_: zeros, pipeline_mode=pl.Buffered(1))


def _params(n_axes):
    return pltpu.CompilerParams(
        dimension_semantics=("arbitrary",) * n_axes,
        vmem_limit_bytes=VMEM_LIMIT)


def _rms(x, gain, n=None):
    n = x.shape[-1] if n is None else n
    inv = lax.rsqrt(jnp.sum(x * x, axis=-1, keepdims=True) * (1.0 / n) + EPS)
    return (x * inv) * gain


def _dot(a, b):
    return jnp.dot(a, b, preferred_element_type=F32)


def _dot_nt(a, b):
    return lax.dot_general(a, b, (((1,), (1,)), ((), ())), preferred_element_type=F32)


def _ffn_body(x_ref, g_ref, wg_ref, wu_ref, wd_ref, o_ref, xn_ref, acc_ref):
    xn_ref[...] = _rms(x_ref[...], g_ref[...]).astype(BF16)
    acc_ref[...] = jnp.zeros_like(acc_ref)

    def chunk(c, carry):
        xn = xn_ref[...]
        g = _dot(xn, wg_ref[c])
        u = _dot(xn, wu_ref[c])
        a = (g * jax.nn.sigmoid(g) * u).astype(BF16)
        acc_ref[...] += _dot(a, wd_ref[c])
        return carry

    lax.fori_loop(0, N_FF_CHUNKS, chunk, 0)
    o_ref[...] = x_ref[...] + 0.5 * acc_ref[...]


def _ffn(x, gain, wg, wu, wd, tm):
    n = x.shape[0]
    return pl.pallas_call(
        _ffn_body,
        grid=(n // tm,),
        in_specs=[
            pl.BlockSpec((tm, D_MODEL), lambda i: (i, 0)),
            _const_spec((1, D_MODEL)),
            _const_spec(wg.shape),
            _const_spec(wu.shape),
            _const_spec(wd.shape),
        ],
        out_specs=pl.BlockSpec((tm, D_MODEL), lambda i: (i, 0)),
        out_shape=jax.ShapeDtypeStruct((n, D_MODEL), F32),
        scratch_shapes=[pltpu.VMEM((tm, D_MODEL), BF16), pltpu.VMEM((tm, D_MODEL), F32)],
        compiler_params=_params(1),
        name="ffn",
    )(x, gain, wg, wu, wd)


def _rope(x, cos, sin_hi, sin_lo):
    return (x * cos + pltpu.roll(x, QK_ROPE // 2, axis=1) * sin_hi
            + pltpu.roll(x, HEAD_PAD - QK_ROPE // 2, axis=1) * sin_lo)


def _qkv_body(x_ref, gmix_ref, wa_ref, gq_ref, wuq_ref, gkv_ref, wuk_ref, wuv_ref,
              gqh_ref, gkh_ref, cos_ref, shi_ref, slo_ref, q_ref, k_ref, v_ref):
    h = _rms(x_ref[...], gmix_ref[...]).astype(BF16)
    c = _dot(h, wa_ref[...])
    cq = _rms(c[:, :Q_LORA], gq_ref[...]).astype(BF16)
    ckv = _rms(c[:, Q_LORA:Q_LORA + KV_LORA], gkv_ref[...]).astype(BF16)
    kr = c[:, Q_LORA + KV_LORA:]
    q = _dot(cq, wuq_ref[...])
    kn = _dot(ckv, wuk_ref[...])
    v_ref[...] = _dot(ckv, wuv_ref[...]).astype(BF16)
    cos, shi, slo = cos_ref[...], shi_ref[...], slo_ref[...]
    scale = QK_HEAD ** -0.5
    for hd in range(MLA_HEADS):
        sl = slice(hd * HEAD_PAD, (hd + 1) * HEAD_PAD)
        qh = _rope(_rms(q[:, sl], gqh_ref[...], QK_HEAD), cos, shi, slo)
        q_ref[:, sl] = (qh * scale).astype(BF16)
        kh = _rope(_rms(kn[:, sl] + kr, gkh_ref[...], QK_HEAD), cos, shi, slo)
        k_ref[:, sl] = kh.astype(BF16)


def _qkv(x1, seq, w, tables, tm):
    n = x1.shape[0]
    tiles_per_seq = seq // tm
    tok = lambda width: pl.BlockSpec((tm, width), lambda i: (i, 0))
    tab = pl.BlockSpec((tm, HEAD_PAD), lambda i: (i % tiles_per_seq, 0))
    consts = [w["g_mix"], w["w_a"], w["g_q"], w["w_uq"], w["g_kv"], w["w_uk"], w["w_uv"],
              w["g_qh"], w["g_kh"]]
    return pl.pallas_call(
        _qkv_body,
        grid=(n // tm,),
        in_specs=[tok(D_MODEL)] + [_const_spec(a.shape) for a in consts] + [tab, tab, tab],
        out_specs=[tok(MLA_HEADS * HEAD_PAD), tok(MLA_HEADS * HEAD_PAD), tok(MLA_HEADS * V_HEAD)],
        out_shape=[jax.ShapeDtypeStruct((n, MLA_HEADS * HEAD_PAD), BF16),
                   jax.ShapeDtypeStruct((n, MLA_HEADS * HEAD_PAD), BF16),
                   jax.ShapeDtypeStruct((n, MLA_HEADS * V_HEAD), BF16)],
        compiler_params=_params(1),
        name="qkv",
    )(x1, *consts, *tables)


def _attn_body(q_ref, k_ref, v_ref, o_ref, acc_ref, *, tk):
    tq = q_ref.shape[0]
    n_kv = k_ref.shape[0] // tk
    lane = lax.broadcasted_iota(jnp.int32, (1, 2 * V_HEAD), 1)
    out = jnp.zeros((tq, 2 * V_HEAD), F32)
    for hh in range(2):
        q = q_ref[:, hh * HEAD_PAD:(hh + 1) * HEAD_PAD]
        keep = (lane < V_HEAD) if hh == 0 else (lane >= V_HEAD)
        acc_ref[...] = jnp.zeros_like(acc_ref)

        def step(j, carry):
            m, l = carry
            rows = pl.ds(pl.multiple_of(j * tk, tk), tk)
            s = _dot_nt(q, k_ref[rows, hh * HEAD_PAD:(hh + 1) * HEAD_PAD])
            m_new = jnp.maximum(m, jnp.max(s, axis=-1, keepdims=True))
            alpha = jnp.exp(m - m_new)
            p = jnp.exp(s - m_new)
            l_new = alpha * l + jnp.sum(p, axis=-1, keepdims=True)
            v = jnp.where(keep, v_ref[rows, :], jnp.zeros((), BF16))
            acc_ref[...] = alpha * acc_ref[...] + _dot(p.astype(BF16), v)
            return m_new, l_new

        m0 = jnp.full((tq, 1), -jnp.inf, F32)
        l0 = jnp.zeros((tq, 1), F32)
        _, l = lax.fori_loop(0, n_kv, step, (m0, l0))
        out = out + acc_ref[...] / l
    o_ref[...] = out.astype(BF16)


def _attn(q, k, v, batch, seq, tq, tk):
    n = q.shape[0]
    nq = seq // tq
    return pl.pallas_call(
        functools.partial(_attn_body, tk=tk),
        grid=(batch, MLA_HEADS // 2, nq),
        in_specs=[
            pl.BlockSpec((tq, 2 * HEAD_PAD), lambda b, hp, i: (b * nq + i, hp)),
            pl.BlockSpec((seq, 2 * HEAD_PAD), lambda b, hp, i: (b, hp)),
            pl.BlockSpec((seq, 2 * V_HEAD), lambda b, hp, i: (b, hp)),
        ],
        out_specs=pl.BlockSpec((tq, 2 * V_HEAD), lambda b, hp, i: (b * nq + i, hp)),
        out_shape=jax.ShapeDtypeStruct((n, MLA_HEADS * V_HEAD), BF16),
        scratch_shapes=[pltpu.VMEM((tq, 2 * V_HEAD), F32)],
        compiler_params=_params(3),
        name="attn",
    )(q, k, v)


def _memkv_body(mem_ref, g_ref, w_ref, gk_ref, k_ref, v_ref):
    m = _rms(mem_ref[...], g_ref[...]).astype(BF16)
    kv = _dot(m, w_ref[...])
    width = XA_HEADS * XA_HEAD
    for hd in range(XA_HEADS):
        sl = slice(hd * XA_HEAD, (hd + 1) * XA_HEAD)
        k_ref[:, sl] = _rms(kv[:, sl], gk_ref[...]).astype(BF16)
    v_ref[...] = kv[:, width:].astype(BF16)


def _memkv(mem, w):
    n = mem.shape[0]
    width = XA_HEADS * XA_HEAD
    blk = lambda cols: pl.BlockSpec((N_MEM, cols), lambda b: (b, 0))
    consts = [w["g_mem"], w["w_mem_kv"], w["g_xk"]]
    return pl.pallas_call(
        _memkv_body,
        grid=(n // N_MEM,),
        in_specs=[blk(D_MODEL)] + [_const_spec(a.shape) for a in consts],
        out_specs=[blk(width), blk(width)],
        out_shape=[jax.ShapeDtypeStruct((n, width), BF16)] * 2,
        compiler_params=_params(1),
        name="memkv",
    )(mem, *consts)


def _mix_body(x_ref, xprev_ref, xnext_ref, o_ref, mk_ref, mv_ref, gmix_ref, wcb_ref, wcc_ref,
              wcx_ref, wxq_ref, wg_ref, convw_ref, gxq_ref, womla_ref, woconv_ref, womem_ref,
              wout_ref, y_ref, u_ref, *, tiles_per_seq):
    i = pl.program_id(0)
    tm = x_ref.shape[0]
    x = x_ref[...]
    h = _rms(x, gmix_ref[...]).astype(BF16)

    def gated(hh):
        return _dot(hh, wcc_ref[...]) * _dot(hh, wcx_ref[...])

    first = (i % tiles_per_seq) == 0
    last = (i % tiles_per_seq) == tiles_per_seq - 1
    h_prev = _rms(xprev_ref[...], gmix_ref[...]).astype(BF16)
    h_next = _rms(xnext_ref[...], gmix_ref[...]).astype(BF16)
    u_ref[0:8, :] = jnp.where(first, 0.0, gated(h_prev))
    u_ref[8:8 + tm, :] = gated(h)
    u_ref[8 + tm:16 + tm, :] = jnp.where(last, 0.0, gated(h_next))
    cw = convw_ref[...]
    conv = (u_ref[7:7 + tm, :] * cw[0:1, :] + u_ref[8:8 + tm, :] * cw[1:2, :]
            + u_ref[9:9 + tm, :] * cw[2:3, :])
    y_conv = _dot((_dot(h, wcb_ref[...]) * conv).astype(BF16), woconv_ref[...])

    xq = _dot(h, wxq_ref[...])
    heads = []
    for hd in range(XA_HEADS):
        sl = slice(hd * XA_HEAD, (hd + 1) * XA_HEAD)
        qh = (_rms(xq[:, sl], gxq_ref[...]) * (XA_HEAD ** -0.5)).astype(BF16)
        s = _dot_nt(qh, mk_ref[:, sl])
        p = jnp.exp(s - jnp.max(s, axis=-1, keepdims=True))
        l = jnp.sum(p, axis=-1, keepdims=True)
        heads.append((_dot(p.astype(BF16), mv_ref[:, sl]) / l).astype(BF16))
    y_mem = _dot(jnp.concatenate(heads, axis=-1), womem_ref[...])

    y_mla = _dot(o_ref[...], womla_ref[...])

    gate = lambda b: jax.nn.sigmoid(_dot(h, wg_ref[b]))
    merged = gate(0) * y_mla + gate(1) * y_conv + gate(2) * y_mem
    y_ref[...] = x + _dot(merged.astype(BF16), wout_ref[...])


def _mix(x1, o, mk, mv, seq, w, tm):
    n = x1.shape[0]
    tiles_per_seq = seq // tm
    halo = tm // 8
    n_halo = n // 8
    tok = lambda width: pl.BlockSpec((tm, width), lambda i: (i, 0))
    mem = pl.BlockSpec((N_MEM, XA_HEADS * XA_HEAD), lambda i: (i // tiles_per_seq, 0))
    consts = [w["g_mix"], w["w_cb"], w["w_cc"], w["w_cx"], w["w_xq"], w["w_gate"], w["conv_w"],
              w["g_xq"], w["w_o_mla"], w["w_o_conv"], w["w_o_mem"], w["w_out"]]
    return pl.pallas_call(
        functools.partial(_mix_body, tiles_per_seq=tiles_per_seq),
        grid=(n // tm,),
        in_specs=[
            tok(D_MODEL),
            pl.BlockSpec((8, D_MODEL), lambda i: (jnp.maximum(i * halo - 1, 0), 0)),
            pl.BlockSpec((8, D_MODEL), lambda i: (jnp.minimum((i + 1) * halo, n_halo - 1), 0)),
            tok(MLA_HEADS * V_HEAD), mem, mem,
        ] + [_const_spec(a.shape) for a in consts],
        out_specs=tok(D_MODEL),
        out_shape=jax.ShapeDtypeStruct((n, D_MODEL), F32),
        scratch_shapes=[pltpu.VMEM((tm + 16, CONV_WIDTH), F32)],
        compiler_params=_params(1),
        name="mix",
    )(x1, x1, x1, o, mk, mv, *consts)


def _pad_heads(w, heads, width):
    k = w.shape[0]
    w = w.reshape(k, heads, width)
    return jnp.pad(w, ((0, 0), (0, 0), (0, HEAD_PAD - width))).reshape(k, heads * HEAD_PAD)


def _ffn_weights(norm, w_gu, w_down):
    chunks = lambda w: w.reshape(D_MODEL, N_FF_CHUNKS, FF_CHUNK).transpose(1, 0, 2).astype(BF16)
    return (norm.reshape(1, D_MODEL), chunks(w_gu[:, :D_FF]), chunks(w_gu[:, D_FF:]),
            w_down.reshape(N_FF_CHUNKS, FF_CHUNK, D_MODEL).astype(BF16))


def _rope_tables(seq):
    half = QK_ROPE // 2
    inv_freq = ROPE_BASE ** (-jnp.arange(half, dtype=F32) / half)
    ang = jnp.arange(seq, dtype=jnp.int32).astype(F32)[:, None] * inv_freq[None, :]
    cos, sin = jnp.cos(ang), jnp.sin(ang)
    zeros = lambda width: jnp.zeros((seq, width), F32)
    tail = HEAD_PAD - QK_HEAD
    cos_t = jnp.concatenate([jnp.ones((seq, QK_NOPE), F32), cos, cos, zeros(tail)], axis=1)
    sin_hi = jnp.concatenate([zeros(QK_NOPE + half), sin, zeros(tail)], axis=1)
    sin_lo = jnp.concatenate([zeros(QK_NOPE), -sin, zeros(half + tail)], axis=1)
    return cos_t, sin_hi, sin_lo


def _trunk(x, mem, w, ffn1, ffn2, tables):
    batch, seq, _ = x.shape
    x = x.reshape(batch * seq, D_MODEL)
    x1 = _ffn(x, *ffn1, tm=512)
    q, k, v = _qkv(x1, seq, w, tables, tm=512)
    o = _attn(q, k, v, batch, seq, tq=256, tk=512)
    mk, mv = _memkv(mem.reshape(batch * N_MEM, D_MODEL), w)
    x2 = _mix(x1, o, mk, mv, seq, w, tm=256)
    y = _ffn(x2, *ffn2, tm=512)
    return y.reshape(batch, seq, D_MODEL)


def kernel(x_prompt, x_sample, mem_prompt, mem_sample, ffn1_norm, ffn1_w_gu, ffn1_w_down, mix_norm, w_in, q_lora_norm, w_uq, kv_lora_norm, w_uk, w_uv, mla_q_norm, mla_k_norm, w_o_mla, conv_w, w_o_conv, mem_norm, w_mem_kv, xa_q_norm, xa_k_norm, w_o_mem, w_out, ffn2_norm, ffn2_w_gu, ffn2_w_down):
    ffn1 = _ffn_weights(ffn1_norm[0], ffn1_w_gu[0], ffn1_w_down[0])
    ffn2 = _ffn_weights(ffn2_norm[0], ffn2_w_gu[0], ffn2_w_down[0])

    w_in0 = w_in[0]
    o_cq, o_ckv, o_kr = 0, Q_LORA, Q_LORA + KV_LORA
    o_cb = o_kr + QK_ROPE
    o_cc, o_cx = o_cb + CONV_WIDTH, o_cb + 2 * CONV_WIDTH
    o_xq = o_cb + 3 * CONV_WIDTH
    o_g = o_xq + XA_HEADS * XA_HEAD
    cols = lambda start, width: w_in0[:, start:start + width]
    kr_pad = jnp.pad(cols(o_kr, QK_ROPE), ((0, 0), (QK_NOPE, HEAD_PAD - QK_HEAD)))
    row = lambda g: g.reshape(1, -1)
    pad_gain = lambda g: jnp.pad(g, (0, HEAD_PAD - QK_HEAD)).reshape(1, HEAD_PAD)
    w = {
        "g_mix": row(mix_norm[0]),
        "w_a": jnp.concatenate([cols(o_cq, Q_LORA), cols(o_ckv, KV_LORA), kr_pad], axis=1).astype(BF16),
        "g_q": row(q_lora_norm[0]),
        "w_uq": _pad_heads(w_uq[0], MLA_HEADS, QK_HEAD).astype(BF16),
        "g_kv": row(kv_lora_norm[0]),
        "w_uk": _pad_heads(w_uk[0], MLA_HEADS, QK_NOPE).astype(BF16),
        "w_uv": w_uv[0].astype(BF16),
        "g_qh": pad_gain(mla_q_norm[0]),
        "g_kh": pad_gain(mla_k_norm[0]),
        "w_cb": cols(o_cb, CONV_WIDTH).astype(BF16),
        "w_cc": cols(o_cc, CONV_WIDTH).astype(BF16),
        "w_cx": cols(o_cx, CONV_WIDTH).astype(BF16),
        "w_xq": cols(o_xq, XA_HEADS * XA_HEAD).astype(BF16),
        "w_gate": cols(o_g, 3 * D_MODEL).reshape(D_MODEL, 3, D_MODEL).transpose(1, 0, 2).astype(BF16),
        "conv_w": conv_w[0],
        "g_xq": row(xa_q_norm[0]),
        "w_o_mla": w_o_mla[0].astype(BF16),
        "w_o_conv": w_o_conv[0].astype(BF16),
        "w_o_mem": w_o_mem[0].astype(BF16),
        "w_out": w_out[0].astype(BF16),
        "g_mem": row(mem_norm[0]),
        "w_mem_kv": w_mem_kv[0].astype(BF16),
        "g_xk": row(xa_k_norm[0]),
    }
    tables = _rope_tables(max(x_prompt.shape[1], x_sample.shape[1]))
    y_prompt = _trunk(x_prompt, mem_prompt, w, ffn1, ffn2, tables)
    y_sample = _trunk(x_sample, mem_sample, w, ffn1, ffn2, tables)
    return (y_prompt, y_sample)
```

```python
import functools

import jax
import jax.numpy as jnp
from jax import lax
from jax.experimental import pallas as pl
from jax.experimental.pallas import tpu as pltpu

D_MODEL = 1024
N_MEM = 256
MLA_HEADS = 8
QK_NOPE = 64
QK_ROPE = 32
QK_HEAD = QK_NOPE + QK_ROPE
V_HEAD = 64
Q_LORA = 384
KV_LORA = 256
CONV_WIDTH = 512
XA_HEADS = 4
XA_HEAD = 128
D_FF = 2816
ROPE_BASE = 10000.0
EPS = 1e-6

LANES = 128
HEAD_PAD = LANES
FF_CHUNK = 256
N_FF_CHUNKS = D_FF // FF_CHUNK
KV_CHUNK = 512
SUM_ROWS = 16
VMEM_LIMIT = 56 * 1024 * 1024

BF16 = jnp.bfloat16
F32 = jnp.float32


def _const_spec(shape):
    zeros = (0,) * len(shape)
    return pl.BlockSpec(shape, lambda *_: zeros, pipeline_mode=pl.Buffered(1))


def _params(n_axes):
    return pltpu.CompilerParams(
        dimension_semantics=("arbitrary",) * n_axes,
        vmem_limit_bytes=VMEM_LIMIT)


def _rms(x, gain, n=None):
    n = x.shape[-1] if n is None else n
    inv = lax.rsqrt(jnp.sum(x * x, axis=-1, keepdims=True) * (1.0 / n) + EPS)
    return (x * inv) * gain


def _dot(a, b):
    return jnp.dot(a, b, preferred_element_type=F32)


def _dot_nt(a, b):
    return lax.dot_general(a, b, (((1,), (1,)), ((), ())), preferred_element_type=F32)


def _ffn_body(x_ref, g_ref, wg_ref, wu_ref, wd_ref, o_ref, xn_ref, acc_ref):
    xn_ref[...] = _rms(x_ref[...], g_ref[...]).astype(BF16)
    acc_ref[...] = jnp.zeros_like(acc_ref)

    def chunk(c, carry):
        xn = xn_ref[...]
        g = _dot(xn, wg_ref[c])
        u = _dot(xn, wu_ref[c])
        a = (g * jax.nn.sigmoid(g) * u).astype(BF16)
        acc_ref[...] += _dot(a, wd_ref[c])
        return carry

    lax.fori_loop(0, N_FF_CHUNKS, chunk, 0)
    o_ref[...] = x_ref[...] + 0.5 * acc_ref[...]


def _ffn(x, gain, wg, wu, wd, tm):
    n = x.shape[0]
    return pl.pallas_call(
        _ffn_body,
        grid=(n // tm,),
        in_specs=[
            pl.BlockSpec((tm, D_MODEL), lambda i: (i, 0)),
            _const_spec((1, D_MODEL)),
            _const_spec(wg.shape),
            _const_spec(wu.shape),
            _const_spec(wd.shape),
        ],
        out_specs=pl.BlockSpec((tm, D_MODEL), lambda i: (i, 0)),
        out_shape=jax.ShapeDtypeStruct((n, D_MODEL), F32),
        scratch_shapes=[pltpu.VMEM((tm, D_MODEL), BF16), pltpu.VMEM((tm, D_MODEL), F32)],
        compiler_params=_params(1),
        name="ffn",
    )(x, gain, wg, wu, wd)


def _rope(x, cos, sin_hi, sin_lo):
    return (x * cos + pltpu.roll(x, QK_ROPE // 2, axis=1) * sin_hi
            + pltpu.roll(x, HEAD_PAD - QK_ROPE // 2, axis=1) * sin_lo)


def _qkv_body(x_ref, gmix_ref, wa_ref, gq_ref, wuqt_ref, gkv_ref, wuk_ref, wuvt_ref,
              gqh_ref, gkh_ref, cos_ref, shi_ref, slo_ref, cost_ref, sint_ref,
              qt_ref, k_ref, vt_ref):
    tm = x_ref.shape[0]
    half = QK_ROPE // 2
    h = _rms(x_ref[...], gmix_ref[...]).astype(BF16)
    c = _dot(h, wa_ref[...])
    cq = _rms(c[:, :Q_LORA], gq_ref[...]).astype(BF16)
    ckv = _rms(c[:, Q_LORA:Q_LORA + KV_LORA], gkv_ref[...]).astype(BF16)
    kr = c[:, Q_LORA + KV_LORA:]

    vt = _dot_nt(wuvt_ref[...], ckv).astype(BF16)
    for ch in range(tm // KV_CHUNK):
        vt_ref[ch] = vt[:, ch * KV_CHUNK:(ch + 1) * KV_CHUNK]

    kn = _dot(ckv, wuk_ref[...])
    cos, shi, slo = cos_ref[...], shi_ref[...], slo_ref[...]
    for hd in range(MLA_HEADS):
        sl = slice(hd * HEAD_PAD, (hd + 1) * HEAD_PAD)
        kh = _rope(_rms(kn[:, sl] + kr, gkh_ref[...], QK_HEAD), cos, shi, slo)
        k_ref[:, sl] = kh.astype(BF16)

    qt = _dot_nt(wuqt_ref[...], cq)
    cost, sint = cost_ref[...], sint_ref[...]
    scale = QK_HEAD ** -0.5
    for hd in range(MLA_HEADS):
        r0 = hd * HEAD_PAD
        blk = qt[r0:r0 + HEAD_PAD, :]
        inv = lax.rsqrt(jnp.sum(blk * blk, axis=0, keepdims=True) * (1.0 / QK_HEAD) + EPS)
        qn = (blk * inv) * gqh_ref[...]
        x1, x2 = qn[QK_NOPE:QK_NOPE + half], qn[QK_NOPE + half:QK_HEAD]
        qt_ref[r0:r0 + QK_NOPE, :] = (qn[:QK_NOPE] * scale).astype(BF16)
        qt_ref[r0 + QK_NOPE:r0 + QK_NOPE + half, :] = ((x1 * cost - x2 * sint) * scale).astype(BF16)
        qt_ref[r0 + QK_NOPE + half:r0 + QK_HEAD, :] = ((x2 * cost + x1 * sint) * scale).astype(BF16)
        qt_ref[r0 + QK_HEAD:r0 + HEAD_PAD, :] = jnp.zeros((HEAD_PAD - QK_HEAD, tm), BF16)


def _qkv(x1, seq, w, tables, tm):
    n = x1.shape[0]
    tiles_per_seq = seq // tm
    tok = lambda width: pl.BlockSpec((tm, width), lambda i: (i, 0))
    tab = pl.BlockSpec((tm, HEAD_PAD), lambda i: (i % tiles_per_seq, 0))
    tab_t = pl.BlockSpec((QK_ROPE // 2, tm), lambda i: (0, i % tiles_per_seq))
    consts = [w["g_mix"], w["w_a"], w["g_q"], w["w_uq_t"], w["g_kv"], w["w_uk"], w["w_uv_t"],
              w["g_qh_col"], w["g_kh"]]
    return pl.pallas_call(
        _qkv_body,
        grid=(n // tm,),
        in_specs=([tok(D_MODEL)] + [_const_spec(a.shape) for a in consts]
                  + [tab, tab, tab, tab_t, tab_t]),
        out_specs=[
            pl.BlockSpec((MLA_HEADS * HEAD_PAD, tm), lambda i: (0, i)),
            tok(MLA_HEADS * HEAD_PAD),
            pl.BlockSpec((tm // KV_CHUNK, MLA_HEADS * V_HEAD, KV_CHUNK), lambda i: (i, 0, 0)),
        ],
        out_shape=[jax.ShapeDtypeStruct((MLA_HEADS * HEAD_PAD, n), BF16),
                   jax.ShapeDtypeStruct((n, MLA_HEADS * HEAD_PAD), BF16),
                   jax.ShapeDtypeStruct((n // KV_CHUNK, MLA_HEADS * V_HEAD, KV_CHUNK), BF16)],
        compiler_params=_params(1),
        name="qkv",
    )(x1, *consts, *tables)


def _attn_body(qt_ref, k_ref, vt_ref, o_ref, s_ref, acc_ref):
    tq = qt_ref.shape[1]
    n_kv = vt_ref.shape[0]
    ones = jnp.ones((SUM_ROWS, KV_CHUNK), BF16)
    acc_ref[...] = jnp.zeros_like(acc_ref)

    def scores(j, slot):
        rows = pl.ds(pl.multiple_of(j * KV_CHUNK, KV_CHUNK), KV_CHUNK)
        cmax = []
        for hh in range(2):
            st = _dot(k_ref[rows, hh * HEAD_PAD:(hh + 1) * HEAD_PAD],
                      qt_ref[hh * HEAD_PAD:(hh + 1) * HEAD_PAD, :])
            s_ref[hh, slot] = st
            cmax.append(jnp.max(st, axis=0, keepdims=True))
        return tuple(cmax)

    def consume(j, slot, ms, cmax):
        out = []
        for hh in range(2):
            m_new = jnp.maximum(ms[hh], cmax[hh])
            alpha = jnp.exp(ms[hh] - m_new)
            pt = jnp.exp(s_ref[hh, slot] - m_new).astype(BF16)
            vt = jnp.concatenate([vt_ref[j, hh * V_HEAD:(hh + 1) * V_HEAD, :], ones], axis=0)
            acc_ref[hh] = alpha * acc_ref[hh] + _dot(vt, pt)
            out.append(m_new)
        return tuple(out)

    def pair(t, carry):
        ms, c_even = carry
        c_odd = scores(2 * t + 1, 1)
        ms = consume(2 * t, 0, ms, c_even)
        c_even = scores(2 * t + 2, 0)
        ms = consume(2 * t + 1, 1, ms, c_odd)
        return ms, c_even

    m0 = jnp.full((1, tq), -jnp.inf, F32)
    ms, c_even = lax.fori_loop(0, n_kv // 2 - 1, pair, ((m0, m0), scores(0, 0)))
    c_odd = scores(n_kv - 1, 1)
    ms = consume(n_kv - 2, 0, ms, c_even)
    consume(n_kv - 1, 1, ms, c_odd)
    heads = [acc_ref[hh, :V_HEAD, :] / acc_ref[hh, V_HEAD:V_HEAD + 1, :] for hh in range(2)]
    o_ref[...] = jnp.concatenate(heads, axis=0).T.astype(BF16)


def _attn(qt, k, vt, batch, seq, tq):
    n = k.shape[0]
    nq = seq // tq
    n_kv = seq // KV_CHUNK
    return pl.pallas_call(
        _attn_body,
        grid=(batch, MLA_HEADS // 2, nq),
        in_specs=[
            pl.BlockSpec((2 * HEAD_PAD, tq), lambda b, hp, i: (hp, b * nq + i)),
            pl.BlockSpec((seq, 2 * HEAD_PAD), lambda b, hp, i: (b, hp)),
            pl.BlockSpec((n_kv, 2 * V_HEAD, KV_CHUNK), lambda b, hp, i: (b, hp, 0)),
        ],
        out_specs=pl.BlockSpec((tq, 2 * V_HEAD), lambda b, hp, i: (b * nq + i, hp)),
        out_shape=jax.ShapeDtypeStruct((n, MLA_HEADS * V_HEAD), BF16),
        scratch_shapes=[pltpu.VMEM((2, 2, KV_CHUNK, tq), F32),
                        pltpu.VMEM((2, V_HEAD + SUM_ROWS, tq), F32)],
        compiler_params=_params(3),
        name="attn",
    )(qt, k, vt)


def _memkv_body(mem_ref, g_ref, w_ref, gk_ref, k_ref, v_ref):
    m = _rms(mem_ref[...], g_ref[...]).astype(BF16)
    kv = _dot(m, w_ref[...])
    width = XA_HEADS * XA_HEAD
    for hd in range(XA_HEADS):
        sl = slice(hd * XA_HEAD, (hd + 1) * XA_HEAD)
        k_ref[:, sl] = _rms(kv[:, sl], gk_ref[...]).astype(BF16)
    v_ref[...] = kv[:, width:].astype(BF16)


def _memkv(mem, w):
    n = mem.shape[0]
    width = XA_HEADS * XA_HEAD
    blk = lambda cols: pl.BlockSpec((N_MEM, cols), lambda b: (b, 0))
    consts = [w["g_mem"], w["w_mem_kv"], w["g_xk"]]
    return pl.pallas_call(
        _memkv_body,
        grid=(n // N_MEM,),
        in_specs=[blk(D_MODEL)] + [_const_spec(a.shape) for a in consts],
        out_specs=[blk(width), blk(width)],
        out_shape=[jax.ShapeDtypeStruct((n, width), BF16)] * 2,
        compiler_params=_params(1),
        name="memkv",
    )(mem, *consts)


def _mix_body(x_ref, xprev_ref, xnext_ref, o_ref, mk_ref, mv_ref, gmix_ref, wcb_ref, wcc_ref,
              wcx_ref, wxq_ref, wg_ref, convw_ref, gxq_ref, womla_ref, woconv_ref, womem_ref,
              wout_ref, y_ref, u_ref, *, tiles_per_seq):
    i = pl.program_id(0)
    tm = x_ref.shape[0]
    x = x_ref[...]
    h = _rms(x, gmix_ref[...]).astype(BF16)

    def gated(hh):
        return _dot(hh, wcc_ref[...]) * _dot(hh, wcx_ref[...])

    first = (i % tiles_per_seq) == 0
    last = (i % tiles_per_seq) == tiles_per_seq - 1
    h_prev = _rms(xprev_ref[...], gmix_ref[...]).astype(BF16)
    h_next = _rms(xnext_ref[...], gmix_ref[...]).astype(BF16)
    u_ref[0:8, :] = jnp.where(first, 0.0, gated(h_prev))
    u_ref[8:8 + tm, :] = gated(h)
    u_ref[8 + tm:16 + tm, :] = jnp.where(last, 0.0, gated(h_next))
    cw = convw_ref[...]
    conv = (u_ref[7:7 + tm, :] * cw[0:1, :] + u_ref[8:8 + tm, :] * cw[1:2, :]
            + u_ref[9:9 + tm, :] * cw[2:3, :])
    y_conv = _dot((_dot(h, wcb_ref[...]) * conv).astype(BF16), woconv_ref[...])

    xq = _dot(h, wxq_ref[...])
    heads = []
    for hd in range(XA_HEADS):
        sl = slice(hd * XA_HEAD, (hd + 1) * XA_HEAD)
        qh = (_rms(xq[:, sl], gxq_ref[...]) * (XA_HEAD ** -0.5)).astype(BF16)
        s = _dot_nt(qh, mk_ref[:, sl])
        p = jnp.exp(s - jnp.max(s, axis=-1, keepdims=True))
        l = jnp.sum(p, axis=-1, keepdims=True)
        heads.append((_dot(p.astype(BF16), mv_ref[:, sl]) / l).astype(BF16))
    y_mem = _dot(jnp.concatenate(heads, axis=-1), womem_ref[...])

    y_mla = _dot(o_ref[...], womla_ref[...])

    gate = lambda b: jax.nn.sigmoid(_dot(h, wg_ref[b]))
    merged = gate(0) * y_mla + gate(1) * y_conv + gate(2) * y_mem
    y_ref[...] = x + _dot(merged.astype(BF16), wout_ref[...])


def _mix(x1, o, mk, mv, seq, w, tm):
    n = x1.shape[0]
    tiles_per_seq = seq // tm
    halo = tm // 8
    n_halo = n // 8
    tok = lambda width: pl.BlockSpec((tm, width), lambda i: (i, 0))
    mem = pl.BlockSpec((N_MEM, XA_HEADS * XA_HEAD), lambda i: (i // tiles_per_seq, 0))
    consts = [w["g_mix"], w["w_cb"], w["w_cc"], w["w_cx"], w["w_xq"], w["w_gate"], w["conv_w"],
              w["g_xq"], w["w_o_mla"], w["w_o_conv"], w["w_o_mem"], w["w_out"]]
    return pl.pallas_call(
        functools.partial(_mix_body, tiles_per_seq=tiles_per_seq),
        grid=(n // tm,),
        in_specs=[
            tok(D_MODEL),
            pl.BlockSpec((8, D_MODEL), lambda i: (jnp.maximum(i * halo - 1, 0), 0)),
            pl.BlockSpec((8, D_MODEL), lambda i: (jnp.minimum((i + 1) * halo, n_halo - 1), 0)),
            tok(MLA_HEADS * V_HEAD), mem, mem,
        ] + [_const_spec(a.shape) for a in consts],
        out_specs=tok(D_MODEL),
        out_shape=jax.ShapeDtypeStruct((n, D_MODEL), F32),
        scratch_shapes=[pltpu.VMEM((tm + 16, CONV_WIDTH), F32)],
        compiler_params=_params(1),
        name="mix",
    )(x1, x1, x1, o, mk, mv, *consts)


def _pad_heads(w, heads, width):
    k = w.shape[0]
    w = w.reshape(k, heads, width)
    return jnp.pad(w, ((0, 0), (0, 0), (0, HEAD_PAD - width))).reshape(k, heads * HEAD_PAD)


def _ffn_weights(norm, w_gu, w_down):
    chunks = lambda w: w.reshape(D_MODEL, N_FF_CHUNKS, FF_CHUNK).transpose(1, 0, 2).astype(BF16)
    return (norm.reshape(1, D_MODEL), chunks(w_gu[:, :D_FF]), chunks(w_gu[:, D_FF:]),
            w_down.reshape(N_FF_CHUNKS, FF_CHUNK, D_MODEL).astype(BF16))


def _rope_tables(seq):
    half = QK_ROPE // 2
    inv_freq = ROPE_BASE ** (-jnp.arange(half, dtype=F32) / half)
    ang = jnp.arange(seq, dtype=jnp.int32).astype(F32)[:, None] * inv_freq[None, :]
    cos, sin = jnp.cos(ang), jnp.sin(ang)
    zeros = lambda width: jnp.zeros((seq, width), F32)
    tail = HEAD_PAD - QK_HEAD
    cos_t = jnp.concatenate([jnp.ones((seq, QK_NOPE), F32), cos, cos, zeros(tail)], axis=1)
    sin_hi = jnp.concatenate([zeros(QK_NOPE + half), sin, zeros(tail)], axis=1)
    sin_lo = jnp.concatenate([zeros(QK_NOPE), -sin, zeros(half + tail)], axis=1)
    return cos_t, sin_hi, sin_lo, cos.T, sin.T


def _trunk(x, mem, w, ffn1, ffn2, tables):
    batch, seq, _ = x.shape
    x = x.reshape(batch * seq, D_MODEL)
    x1 = _ffn(x, *ffn1, tm=512)
    qt, k, vt = _qkv(x1, seq, w, tables, tm=512)
    o = _attn(qt, k, vt, batch, seq, tq=256)
    mk, mv = _memkv(mem.reshape(batch * N_MEM, D_MODEL), w)
    x2 = _mix(x1, o, mk, mv, seq, w, tm=256)
    y = _ffn(x2, *ffn2, tm=512)
    return y.reshape(batch, seq, D_MODEL)


def kernel(x_prompt, x_sample, mem_prompt, mem_sample, ffn1_norm, ffn1_w_gu, ffn1_w_down, mix_norm, w_in, q_lora_norm, w_uq, kv_lora_norm, w_uk, w_uv, mla_q_norm, mla_k_norm, w_o_mla, conv_w, w_o_conv, mem_norm, w_mem_kv, xa_q_norm, xa_k_norm, w_o_mem, w_out, ffn2_norm, ffn2_w_gu, ffn2_w_down):
    ffn1 = _ffn_weights(ffn1_norm[0], ffn1_w_gu[0], ffn1_w_down[0])
    ffn2 = _ffn_weights(ffn2_norm[0], ffn2_w_gu[0], ffn2_w_down[0])

    w_in0 = w_in[0]
    o_cq, o_ckv, o_kr = 0, Q_LORA, Q_LORA + KV_LORA
    o_cb = o_kr + QK_ROPE
    o_cc, o_cx = o_cb + CONV_WIDTH, o_cb + 2 * CONV_WIDTH
    o_xq = o_cb + 3 * CONV_WIDTH
    o_g = o_xq + XA_HEADS * XA_HEAD
    cols = lambda start, width: w_in0[:, start:start + width]
    kr_pad = jnp.pad(cols(o_kr, QK_ROPE), ((0, 0), (QK_NOPE, HEAD_PAD - QK_HEAD)))
    row = lambda g: g.reshape(1, -1)
    pad_gain = lambda g: jnp.pad(g, (0, HEAD_PAD - QK_HEAD))
    w = {
        "g_mix": row(mix_norm[0]),
        "w_a": jnp.concatenate([cols(o_cq, Q_LORA), cols(o_ckv, KV_LORA), kr_pad], axis=1).astype(BF16),
        "g_q": row(q_lora_norm[0]),
        "w_uq_t": _pad_heads(w_uq[0], MLA_HEADS, QK_HEAD).T.astype(BF16),
        "g_kv": row(kv_lora_norm[0]),
        "w_uk": _pad_heads(w_uk[0], MLA_HEADS, QK_NOPE).astype(BF16),
        "w_uv_t": w_uv[0].T.astype(BF16),
        "g_qh_col": pad_gain(mla_q_norm[0]).reshape(HEAD_PAD, 1),
        "g_kh": pad_gain(mla_k_norm[0]).reshape(1, HEAD_PAD),
        "w_cb": cols(o_cb, CONV_WIDTH).astype(BF16),
        "w_cc": cols(o_cc, CONV_WIDTH).astype(BF16),
        "w_cx": cols(o_cx, CONV_WIDTH).astype(BF16),
        "w_xq": cols(o_xq, XA_HEADS * XA_HEAD).astype(BF16),
        "w_gate": cols(o_g, 3 * D_MODEL).reshape(D_MODEL, 3, D_MODEL).transpose(1, 0, 2).astype(BF16),
        "conv_w": conv_w[0],
        "g_xq": row(xa_q_norm[0]),
        "w_o_mla": w_o_mla[0].astype(BF16),
        "w_o_conv": w_o_conv[0].astype(BF16),
        "w_o_mem": w_o_mem[0].astype(BF16),
        "w_out": w_out[0].astype(BF16),
        "g_mem": row(mem_norm[0]),
        "w_mem_kv": w_mem_kv[0].astype(BF16),
        "g_xk": row(xa_k_norm[0]),
    }
    tables = _rope_tables(max(x_prompt.shape[1], x_sample.shape[1]))
    y_prompt = _trunk(x_prompt, mem_prompt, w, ffn1, ffn2, tables)
    y_sample = _trunk(x_sample, mem_sample, w, ffn1, ffn2, tables)
    return (y_prompt, y_sample)
```

```python
import functools

import jax
import jax.numpy as jnp
from jax import lax
from jax.experimental import pallas as pl
from jax.experimental.pallas import tpu as pltpu

D_MODEL = 1024
N_MEM = 256
MLA_HEADS = 8
QK_NOPE = 64
QK_ROPE = 32
QK_HEAD = QK_NOPE + QK_ROPE
V_HEAD = 64
Q_LORA = 384
KV_LORA = 256
CONV_WIDTH = 512
XA_HEADS = 4
XA_HEAD = 128
D_FF = 2816
ROPE_BASE = 10000.0
EPS = 1e-6
LOG2_E = 1.4426950408889634

LANES = 128
HEAD_PAD = LANES
FF_CHUNK = 256
N_FF_CHUNKS = D_FF // FF_CHUNK
KV_CHUNK = 512
Q_TILE = 256
SUM_ROWS = 16
VMEM_LIMIT = 56 * 1024 * 1024

BF16 = jnp.bfloat16
F32 = jnp.float32


def _const_spec(shape):
    zeros = (0,) * len(shape)
    return pl.BlockSpec(shape, lambda *_: zeros, pipeline_mode=pl.Buffered(1))


def _params(n_axes):
    return pltpu.CompilerParams(
        dimension_semantics=("arbitrary",) * n_axes,
        vmem_limit_bytes=VMEM_LIMIT)


def _rms(x, gain, n=None):
    n = x.shape[-1] if n is None else n
    inv = lax.rsqrt(jnp.sum(x * x, axis=-1, keepdims=True) * (1.0 / n) + EPS)
    return (x * inv) * gain


def _dot(a, b):
    return jnp.dot(a, b, preferred_element_type=F32)


def _dot_nt(a, b):
    return lax.dot_general(a, b, (((1,), (1,)), ((), ())), preferred_element_type=F32)


def _ffn_body(x_ref, g_ref, wg_ref, wu_ref, wd_ref, o_ref, xn_ref, acc_ref):
    xn_ref[...] = _rms(x_ref[...], g_ref[...]).astype(BF16)
    acc_ref[...] = jnp.zeros_like(acc_ref)

    def chunk(c, carry):
        xn = xn_ref[...]
        g = _dot(xn, wg_ref[c])
        u = _dot(xn, wu_ref[c])
        a = (g * jax.nn.sigmoid(g) * u).astype(BF16)
        acc_ref[...] += _dot(a, wd_ref[c])
        return carry

    lax.fori_loop(0, N_FF_CHUNKS, chunk, 0)
    o_ref[...] = x_ref[...] + 0.5 * acc_ref[...]


def _ffn(x, gain, wg, wu, wd, tm):
    n = x.shape[0]
    return pl.pallas_call(
        _ffn_body,
        grid=(n // tm,),
        in_specs=[
            pl.BlockSpec((tm, D_MODEL), lambda i: (i, 0)),
            _const_spec((1, D_MODEL)),
            _const_spec(wg.shape),
            _const_spec(wu.shape),
            _const_spec(wd.shape),
        ],
        out_specs=pl.BlockSpec((tm, D_MODEL), lambda i: (i, 0)),
        out_shape=jax.ShapeDtypeStruct((n, D_MODEL), F32),
        scratch_shapes=[pltpu.VMEM((tm, D_MODEL), BF16), pltpu.VMEM((tm, D_MODEL), F32)],
        compiler_params=_params(1),
        name="ffn",
    )(x, gain, wg, wu, wd)


def _rope(x, cos, sin_hi, sin_lo):
    return (x * cos + pltpu.roll(x, QK_ROPE // 2, axis=1) * sin_hi
            + pltpu.roll(x, HEAD_PAD - QK_ROPE // 2, axis=1) * sin_lo)


def _qkv_body(x_ref, gmix_ref, wa_ref, gq_ref, wuqt_ref, gkv_ref, wuk_ref, wuvt_ref,
              gqh_ref, gkh_ref, cos_ref, shi_ref, slo_ref, cost_ref, sint_ref,
              qt_ref, k_ref, vt_ref):
    tm = x_ref.shape[0]
    half = QK_ROPE // 2
    h = _rms(x_ref[...], gmix_ref[...]).astype(BF16)
    c = _dot(h, wa_ref[...])
    cq = _rms(c[:, :Q_LORA], gq_ref[...]).astype(BF16)
    ckv = _rms(c[:, Q_LORA:Q_LORA + KV_LORA], gkv_ref[...]).astype(BF16)
    kr = c[:, Q_LORA + KV_LORA:]

    vt = _dot_nt(wuvt_ref[...], ckv).astype(BF16)
    for ch in range(tm // KV_CHUNK):
        vt_ref[ch] = vt[:, ch * KV_CHUNK:(ch + 1) * KV_CHUNK]

    kn = _dot(ckv, wuk_ref[...])
    cos, shi, slo = cos_ref[...], shi_ref[...], slo_ref[...]
    for hd in range(MLA_HEADS):
        sl = slice(hd * HEAD_PAD, (hd + 1) * HEAD_PAD)
        kh = _rope(_rms(kn[:, sl] + kr, gkh_ref[...], QK_HEAD), cos, shi, slo)
        k_ref[:, sl] = kh.astype(BF16)

    qt = _dot_nt(wuqt_ref[...], cq)
    cost, sint = cost_ref[...], sint_ref[...]
    scale = QK_HEAD ** -0.5 * LOG2_E

    def put(row, rows_f32):
        for t in range(tm // Q_TILE):
            qt_ref[t, row:row + rows_f32.shape[0], :] = (
                rows_f32[:, t * Q_TILE:(t + 1) * Q_TILE].astype(BF16))

    for hd in range(MLA_HEADS):
        r0 = hd * HEAD_PAD
        blk = qt[r0:r0 + HEAD_PAD, :]
        inv = lax.rsqrt(jnp.sum(blk * blk, axis=0, keepdims=True) * (1.0 / QK_HEAD) + EPS)
        qn = (blk * inv) * gqh_ref[...]
        x1, x2 = qn[QK_NOPE:QK_NOPE + half], qn[QK_NOPE + half:QK_HEAD]
        put(r0, qn[:QK_NOPE] * scale)
        put(r0 + QK_NOPE, (x1 * cost - x2 * sint) * scale)
        put(r0 + QK_NOPE + half, (x2 * cost + x1 * sint) * scale)
        put(r0 + QK_HEAD, jnp.zeros((HEAD_PAD - QK_HEAD, tm), F32))


def _qkv(x1, seq, w, tables, tm):
    n = x1.shape[0]
    tiles_per_seq = seq // tm
    tok = lambda width: pl.BlockSpec((tm, width), lambda i: (i, 0))
    tab = pl.BlockSpec((tm, HEAD_PAD), lambda i: (i % tiles_per_seq, 0))
    tab_t = pl.BlockSpec((QK_ROPE // 2, tm), lambda i: (0, i % tiles_per_seq))
    consts = [w["g_mix"], w["w_a"], w["g_q"], w["w_uq_t"], w["g_kv"], w["w_uk"], w["w_uv_t"],
              w["g_qh_col"], w["g_kh"]]
    return pl.pallas_call(
        _qkv_body,
        grid=(n // tm,),
        in_specs=([tok(D_MODEL)] + [_const_spec(a.shape) for a in consts]
                  + [tab, tab, tab, tab_t, tab_t]),
        out_specs=[
            pl.BlockSpec((tm // Q_TILE, MLA_HEADS * HEAD_PAD, Q_TILE), lambda i: (i, 0, 0)),
            tok(MLA_HEADS * HEAD_PAD),
            pl.BlockSpec((tm // KV_CHUNK, MLA_HEADS * V_HEAD, KV_CHUNK), lambda i: (i, 0, 0)),
        ],
        out_shape=[jax.ShapeDtypeStruct((n // Q_TILE, MLA_HEADS * HEAD_PAD, Q_TILE), BF16),
                   jax.ShapeDtypeStruct((n, MLA_HEADS * HEAD_PAD), BF16),
                   jax.ShapeDtypeStruct((n // KV_CHUNK, MLA_HEADS * V_HEAD, KV_CHUNK), BF16)],
        compiler_params=_params(1),
        name="qkv",
    )(x1, *consts, *tables)


def _attn_body(qt_ref, k_ref, vt_ref, o_ref, s_ref, acc_ref):
    nq, _, tq = qt_ref.shape
    n_kv = vt_ref.shape[0]
    ones = jnp.ones((SUM_ROWS, KV_CHUNK), BF16)
    m0 = jnp.full((1, tq), -jnp.inf, F32)

    def scores(qi, j, slot):
        rows = pl.ds(pl.multiple_of(j * KV_CHUNK, KV_CHUNK), KV_CHUNK)
        cmax = []
        for hh in range(2):
            st = _dot(k_ref[rows, hh * HEAD_PAD:(hh + 1) * HEAD_PAD],
                      qt_ref[qi, hh * HEAD_PAD:(hh + 1) * HEAD_PAD, :])
            s_ref[hh, slot] = st
            cmax.append(jnp.max(st, axis=0, keepdims=True))
        return tuple(cmax)

    def consume(j, slot, ms, cmax):
        out = []
        for hh in range(2):
            m_new = jnp.maximum(ms[hh], cmax[hh])
            alpha = jnp.exp2(ms[hh] - m_new)
            pt = jnp.exp2(s_ref[hh, slot] - m_new).astype(BF16)
            vt = jnp.concatenate([vt_ref[j, hh * V_HEAD:(hh + 1) * V_HEAD, :], ones], axis=0)
            acc_ref[hh] = alpha * acc_ref[hh] + _dot(vt, pt)
            out.append(m_new)
        return tuple(out)

    def finish(qi):
        heads = [acc_ref[hh, :V_HEAD, :] / acc_ref[hh, V_HEAD:V_HEAD + 1, :] for hh in range(2)]
        rows = pl.ds(pl.multiple_of(qi * tq, tq), tq)
        o_ref[rows, :] = jnp.concatenate(heads, axis=0).T.astype(BF16)
        acc_ref[...] = jnp.zeros_like(acc_ref)

    def q_tile(qi, c_even):
        def pair(t, carry):
            ms, c_even = carry
            c_odd = scores(qi, 2 * t + 1, 1)
            ms = consume(2 * t, 0, ms, c_even)
            c_even = scores(qi, 2 * t + 2, 0)
            ms = consume(2 * t + 1, 1, ms, c_odd)
            return ms, c_even

        ms, c_even = lax.fori_loop(0, n_kv // 2 - 1, pair, ((m0, m0), c_even), unroll=2)
        c_odd = scores(qi, n_kv - 1, 1)
        ms = consume(n_kv - 2, 0, ms, c_even)
        c_next = scores(jnp.minimum(qi + 1, nq - 1), 0, 0)
        consume(n_kv - 1, 1, ms, c_odd)
        finish(qi)
        return c_next

    acc_ref[...] = jnp.zeros_like(acc_ref)
    lax.fori_loop(0, nq, q_tile, scores(0, 0, 0))


def _attn(qt, k, vt, batch, seq):
    n = k.shape[0]
    nq = seq // Q_TILE
    n_kv = seq // KV_CHUNK
    return pl.pallas_call(
        _attn_body,
        grid=(batch, MLA_HEADS // 2),
        in_specs=[
            pl.BlockSpec((nq, 2 * HEAD_PAD, Q_TILE), lambda b, hp: (b, hp, 0)),
            pl.BlockSpec((seq, 2 * HEAD_PAD), lambda b, hp: (b, hp)),
            pl.BlockSpec((n_kv, 2 * V_HEAD, KV_CHUNK), lambda b, hp: (b, hp, 0)),
        ],
        out_specs=pl.BlockSpec((seq, 2 * V_HEAD), lambda b, hp: (b, hp)),
        out_shape=jax.ShapeDtypeStruct((n, MLA_HEADS * V_HEAD), BF16),
        scratch_shapes=[pltpu.VMEM((2, 2, KV_CHUNK, Q_TILE), F32),
                        pltpu.VMEM((2, V_HEAD + SUM_ROWS, Q_TILE), F32)],
        compiler_params=_params(2),
        name="attn",
    )(qt, k, vt)


def _memkv_body(mem_ref, g_ref, w_ref, gk_ref, k_ref, v_ref):
    m = _rms(mem_ref[...], g_ref[...]).astype(BF16)
    kv = _dot(m, w_ref[...])
    width = XA_HEADS * XA_HEAD
    for hd in range(XA_HEADS):
        sl = slice(hd * XA_HEAD, (hd + 1) * XA_HEAD)
        k_ref[:, sl] = _rms(kv[:, sl], gk_ref[...]).astype(BF16)
    v_ref[...] = kv[:, width:].astype(BF16)


def _memkv(mem, w):
    n = mem.shape[0]
    width = XA_HEADS * XA_HEAD
    blk = lambda cols: pl.BlockSpec((N_MEM, cols), lambda b: (b, 0))
    consts = [w["g_mem"], w["w_mem_kv"], w["g_xk"]]
    return pl.pallas_call(
        _memkv_body,
        grid=(n // N_MEM,),
        in_specs=[blk(D_MODEL)] + [_const_spec(a.shape) for a in consts],
        out_specs=[blk(width), blk(width)],
        out_shape=[jax.ShapeDtypeStruct((n, width), BF16)] * 2,
        compiler_params=_params(1),
        name="memkv",
    )(mem, *consts)


def _mix_body(x_ref, xprev_ref, xnext_ref, o_ref, mk_ref, mv_ref, gmix_ref, wcb_ref, wcc_ref,
              wcx_ref, wxq_ref, wg_ref, convw_ref, gxq_ref, womla_ref, woconv_ref, womem_ref,
              wout_ref, y_ref, u_ref, *, tiles_per_seq):
    i = pl.program_id(0)
    tm = x_ref.shape[0]
    x = x_ref[...]
    h = _rms(x, gmix_ref[...]).astype(BF16)

    def gated(hh):
        return _dot(hh, wcc_ref[...]) * _dot(hh, wcx_ref[...])

    first = (i % tiles_per_seq) == 0
    last = (i % tiles_per_seq) == tiles_per_seq - 1
    h_prev = _rms(xprev_ref[...], gmix_ref[...]).astype(BF16)
    h_next = _rms(xnext_ref[...], gmix_ref[...]).astype(BF16)
    u_ref[0:8, :] = jnp.where(first, 0.0, gated(h_prev))
    u_ref[8:8 + tm, :] = gated(h)
    u_ref[8 + tm:16 + tm, :] = jnp.where(last, 0.0, gated(h_next))
    cw = convw_ref[...]
    conv = (u_ref[7:7 + tm, :] * cw[0:1, :] + u_ref[8:8 + tm, :] * cw[1:2, :]
            + u_ref[9:9 + tm, :] * cw[2:3, :])
    y_conv = _dot((_dot(h, wcb_ref[...]) * conv).astype(BF16), woconv_ref[...])

    xq = _dot(h, wxq_ref[...])
    heads = []
    for hd in range(XA_HEADS):
        sl = slice(hd * XA_HEAD, (hd + 1) * XA_HEAD)
        qh = (_rms(xq[:, sl], gxq_ref[...]) * (XA_HEAD ** -0.5)).astype(BF16)
        s = _dot_nt(qh, mk_ref[:, sl])
        p = jnp.exp(s - jnp.max(s, axis=-1, keepdims=True))
        l = jnp.sum(p, axis=-1, keepdims=True)
        heads.append((_dot(p.astype(BF16), mv_ref[:, sl]) / l).astype(BF16))
    y_mem = _dot(jnp.concatenate(heads, axis=-1), womem_ref[...])

    y_mla = _dot(o_ref[...], womla_ref[...])

    gate = lambda b: jax.nn.sigmoid(_dot(h, wg_ref[b]))
    merged = gate(0) * y_mla + gate(1) * y_conv + gate(2) * y_mem
    y_ref[...] = x + _dot(merged.astype(BF16), wout_ref[...])


def _mix(x1, o, mk, mv, seq, w, tm):
    n = x1.shape[0]
    tiles_per_seq = seq // tm
    halo = tm // 8
    n_halo = n // 8
    tok = lambda width: pl.BlockSpec((tm, width), lambda i: (i, 0))
    mem = pl.BlockSpec((N_MEM, XA_HEADS * XA_HEAD), lambda i: (i // tiles_per_seq, 0))
    consts = [w["g_mix"], w["w_cb"], w["w_cc"], w["w_cx"], w["w_xq"], w["w_gate"], w["conv_w"],
              w["g_xq"], w["w_o_mla"], w["w_o_conv"], w["w_o_mem"], w["w_out"]]
    return pl.pallas_call(
        functools.partial(_mix_body, tiles_per_seq=tiles_per_seq),
        grid=(n // tm,),
        in_specs=[
            tok(D_MODEL),
            pl.BlockSpec((8, D_MODEL), lambda i: (jnp.maximum(i * halo - 1, 0), 0)),
            pl.BlockSpec((8, D_MODEL), lambda i: (jnp.minimum((i + 1) * halo, n_halo - 1), 0)),
            tok(MLA_HEADS * V_HEAD), mem, mem,
        ] + [_const_spec(a.shape) for a in consts],
        out_specs=tok(D_MODEL),
        out_shape=jax.ShapeDtypeStruct((n, D_MODEL), F32),
        scratch_shapes=[pltpu.VMEM((tm + 16, CONV_WIDTH), F32)],
        compiler_params=_params(1),
        name="mix",
    )(x1, x1, x1, o, mk, mv, *consts)


def _pad_heads(w, heads, width):
    k = w.shape[0]
    w = w.reshape(k, heads, width)
    return jnp.pad(w, ((0, 0), (0, 0), (0, HEAD_PAD - width))).reshape(k, heads * HEAD_PAD)


def _ffn_weights(norm, w_gu, w_down):
    chunks = lambda w: w.reshape(D_MODEL, N_FF_CHUNKS, FF_CHUNK).transpose(1, 0, 2).astype(BF16)
    return (norm.reshape(1, D_MODEL), chunks(w_gu[:, :D_FF]), chunks(w_gu[:, D_FF:]),
            w_down.reshape(N_FF_CHUNKS, FF_CHUNK, D_MODEL).astype(BF16))


def _rope_tables(seq):
    half = QK_ROPE // 2
    inv_freq = ROPE_BASE ** (-jnp.arange(half, dtype=F32) / half)
    ang = jnp.arange(seq, dtype=jnp.int32).astype(F32)[:, None] * inv_freq[None, :]
    cos, sin = jnp.cos(ang), jnp.sin(ang)
    zeros = lambda width: jnp.zeros((seq, width), F32)
    tail = HEAD_PAD - QK_HEAD
    cos_t = jnp.concatenate([jnp.ones((seq, QK_NOPE), F32), cos, cos, zeros(tail)], axis=1)
    sin_hi = jnp.concatenate([zeros(QK_NOPE + half), sin, zeros(tail)], axis=1)
    sin_lo = jnp.concatenate([zeros(QK_NOPE), -sin, zeros(half + tail)], axis=1)
    return cos_t, sin_hi, sin_lo, cos.T, sin.T


def _trunk(x, mem, w, ffn1, ffn2, tables):
    batch, seq, _ = x.shape
    x = x.reshape(batch * seq, D_MODEL)
    x1 = _ffn(x, *ffn1, tm=512)
    qt, k, vt = _qkv(x1, seq, w, tables, tm=512)
    o = _attn(qt, k, vt, batch, seq)
    mk, mv = _memkv(mem.reshape(batch * N_MEM, D_MODEL), w)
    x2 = _mix(x1, o, mk, mv, seq, w, tm=256)
    y = _ffn(x2, *ffn2, tm=512)
    return y.reshape(batch, seq, D_MODEL)


def kernel(x_prompt, x_sample, mem_prompt, mem_sample, ffn1_norm, ffn1_w_gu, ffn1_w_down, mix_norm, w_in, q_lora_norm, w_uq, kv_lora_norm, w_uk, w_uv, mla_q_norm, mla_k_norm, w_o_mla, conv_w, w_o_conv, mem_norm, w_mem_kv, xa_q_norm, xa_k_norm, w_o_mem, w_out, ffn2_norm, ffn2_w_gu, ffn2_w_down):
    ffn1 = _ffn_weights(ffn1_norm[0], ffn1_w_gu[0], ffn1_w_down[0])
    ffn2 = _ffn_weights(ffn2_norm[0], ffn2_w_gu[0], ffn2_w_down[0])

    w_in0 = w_in[0]
    o_cq, o_ckv, o_kr = 0, Q_LORA, Q_LORA + KV_LORA
    o_cb = o_kr + QK_ROPE
    o_cc, o_cx = o_cb + CONV_WIDTH, o_cb + 2 * CONV_WIDTH
    o_xq = o_cb + 3 * CONV_WIDTH
    o_g = o_xq + XA_HEADS * XA_HEAD
    cols = lambda start, width: w_in0[:, start:start + width]
    kr_pad = jnp.pad(cols(o_kr, QK_ROPE), ((0, 0), (QK_NOPE, HEAD_PAD - QK_HEAD)))
    row = lambda g: g.reshape(1, -1)
    pad_gain = lambda g: jnp.pad(g, (0, HEAD_PAD - QK_HEAD))
    w = {
        "g_mix": row(mix_norm[0]),
        "w_a": jnp.concatenate([cols(o_cq, Q_LORA), cols(o_ckv, KV_LORA), kr_pad], axis=1).astype(BF16),
        "g_q": row(q_lora_norm[0]),
        "w_uq_t": _pad_heads(w_uq[0], MLA_HEADS, QK_HEAD).T.astype(BF16),
        "g_kv": row(kv_lora_norm[0]),
        "w_uk": _pad_heads(w_uk[0], MLA_HEADS, QK_NOPE).astype(BF16),
        "w_uv_t": w_uv[0].T.astype(BF16),
        "g_qh_col": pad_gain(mla_q_norm[0]).reshape(HEAD_PAD, 1),
        "g_kh": pad_gain(mla_k_norm[0]).reshape(1, HEAD_PAD),
        "w_cb": cols(o_cb, CONV_WIDTH).astype(BF16),
        "w_cc": cols(o_cc, CONV_WIDTH).astype(BF16),
        "w_cx": cols(o_cx, CONV_WIDTH).astype(BF16),
        "w_xq": cols(o_xq, XA_HEADS * XA_HEAD).astype(BF16),
        "w_gate": cols(o_g, 3 * D_MODEL).reshape(D_MODEL, 3, D_MODEL).transpose(1, 0, 2).astype(BF16),
        "conv_w": conv_w[0],
        "g_xq": row(xa_q_norm[0]),
        "w_o_mla": w_o_mla[0].astype(BF16),
        "w_o_conv": w_o_conv[0].astype(BF16),
        "w_o_mem": w_o_mem[0].astype(BF16),
        "w_out": w_out[0].astype(BF16),
        "g_mem": row(mem_norm[0]),
        "w_mem_kv": w_mem_kv[0].astype(BF16),
        "g_xk": row(xa_k_norm[0]),
    }
    tables = _rope_tables(max(x_prompt.shape[1], x_sample.shape[1]))
    y_prompt = _trunk(x_prompt, mem_prompt, w, ffn1, ffn2, tables)
    y_sample = _trunk(x_sample, mem_sample, w, ffn1, ffn2, tables)
    return (y_prompt, y_sample)
```

```python
import functools

import jax
import jax.numpy as jnp
from jax import lax
from jax.experimental import pallas as pl
from jax.experimental.pallas import tpu as pltpu

D_MODEL = 1024
N_MEM = 256
MLA_HEADS = 8
QK_NOPE = 64
QK_ROPE = 32
QK_HEAD = QK_NOPE + QK_ROPE
V_HEAD = 64
Q_LORA = 384
KV_LORA = 256
CONV_WIDTH = 512
XA_HEADS = 4
XA_HEAD = 128
D_FF = 2816
ROPE_BASE = 10000.0
EPS = 1e-6
LOG2_E = 1.4426950408889634

LANES = 128
HEAD_PAD = LANES
FF_CHUNK = 256
N_FF_CHUNKS = D_FF // FF_CHUNK
KV_CHUNK = 512
Q_TILE = 512
SUM_ROWS = 16
VMEM_LIMIT = 56 * 1024 * 1024

BF16 = jnp.bfloat16
F32 = jnp.float32


def _const_spec(shape):
    zeros = (0,) * len(shape)
    return pl.BlockSpec(shape, lambda *_: zeros, pipeline_mode=pl.Buffered(1))


def _params(n_axes):
    return pltpu.CompilerParams(
        dimension_semantics=("arbitrary",) * n_axes,
        vmem_limit_bytes=VMEM_LIMIT)


def _rms(x, gain, n=None):
    n = x.shape[-1] if n is None else n
    inv = lax.rsqrt(jnp.sum(x * x, axis=-1, keepdims=True) * (1.0 / n) + EPS)
    return (x * inv) * gain


def _dot(a, b):
    return jnp.dot(a, b, preferred_element_type=F32)


def _dot_nt(a, b):
    return lax.dot_general(a, b, (((1,), (1,)), ((), ())), preferred_element_type=F32)


def _ffn_body(x_ref, g_ref, wg_ref, wu_ref, wd_ref, o_ref, xn_ref, acc_ref):
    xn_ref[...] = _rms(x_ref[...], g_ref[...]).astype(BF16)
    acc_ref[...] = jnp.zeros_like(acc_ref)

    def chunk(c, carry):
        xn = xn_ref[...]
        g = _dot(xn, wg_ref[c])
        u = _dot(xn, wu_ref[c])
        a = (g * jax.nn.sigmoid(g) * u).astype(BF16)
        acc_ref[...] += _dot(a, wd_ref[c])
        return carry

    lax.fori_loop(0, N_FF_CHUNKS, chunk, 0, unroll=True)
    o_ref[...] = x_ref[...] + 0.5 * acc_ref[...]


def _ffn(x, gain, wg, wu, wd, tm):
    n = x.shape[0]
    return pl.pallas_call(
        _ffn_body,
        grid=(n // tm,),
        in_specs=[
            pl.BlockSpec((tm, D_MODEL), lambda i: (i, 0)),
            _const_spec((1, D_MODEL)),
            _const_spec(wg.shape),
            _const_spec(wu.shape),
            _const_spec(wd.shape),
        ],
        out_specs=pl.BlockSpec((tm, D_MODEL), lambda i: (i, 0)),
        out_shape=jax.ShapeDtypeStruct((n, D_MODEL), F32),
        scratch_shapes=[pltpu.VMEM((tm, D_MODEL), BF16), pltpu.VMEM((tm, D_MODEL), F32)],
        compiler_params=_params(1),
        name="ffn",
    )(x, gain, wg, wu, wd)


def _qkv_body(x_ref, gmix_ref, wa_ref, gq_ref, wuqt_ref, gkv_ref, wuk_ref, wuvt_ref,
              gqh_ref, gkh_ref, gkhs_ref, cos_ref, sin_ref, cost_ref, sint_ref,
              qt_ref, k_ref, vt_ref):
    tm = x_ref.shape[0]
    half = QK_ROPE // 2
    h = _rms(x_ref[...], gmix_ref[...]).astype(BF16)
    c = _dot(h, wa_ref[...])
    cq = _rms(c[:, :Q_LORA], gq_ref[...]).astype(BF16)
    ckv = _rms(c[:, Q_LORA:Q_LORA + KV_LORA], gkv_ref[...]).astype(BF16)
    o_kr = Q_LORA + KV_LORA
    kr = c[:, o_kr:o_kr + HEAD_PAD]
    kr_swap = c[:, o_kr + HEAD_PAD:]

    vt = _dot_nt(wuvt_ref[...], ckv).astype(BF16)
    for ch in range(tm // KV_CHUNK):
        vt_ref[ch] = vt[:, ch * KV_CHUNK:(ch + 1) * KV_CHUNK]

    kn = _dot(ckv, wuk_ref[...])
    cos, sin = cos_ref[...], sin_ref[...]
    for hd in range(MLA_HEADS):
        sl = slice(hd * HEAD_PAD, (hd + 1) * HEAD_PAD)
        kh = kn[:, sl] + kr
        inv = lax.rsqrt(jnp.sum(kh * kh, axis=-1, keepdims=True) * (1.0 / QK_HEAD) + EPS)
        rot = ((kh * inv) * gkh_ref[...]) * cos + ((kr_swap * inv) * gkhs_ref[...]) * sin
        k_ref[:, sl] = rot.astype(BF16)

    qt = _dot_nt(wuqt_ref[...], cq)
    cost, sint = cost_ref[...], sint_ref[...]
    scale = QK_HEAD ** -0.5 * LOG2_E

    def put(row, rows_f32):
        for t in range(tm // Q_TILE):
            qt_ref[t, row:row + rows_f32.shape[0], :] = (
                rows_f32[:, t * Q_TILE:(t + 1) * Q_TILE].astype(BF16))

    for hd in range(MLA_HEADS):
        r0 = hd * HEAD_PAD
        blk = qt[r0:r0 + HEAD_PAD, :]
        inv = lax.rsqrt(jnp.sum(blk * blk, axis=0, keepdims=True) * (1.0 / QK_HEAD) + EPS)
        qn = (blk * inv) * gqh_ref[...]
        x1, x2 = qn[QK_NOPE:QK_NOPE + half], qn[QK_NOPE + half:QK_HEAD]
        put(r0, qn[:QK_NOPE] * scale)
        put(r0 + QK_NOPE, (x1 * cost - x2 * sint) * scale)
        put(r0 + QK_NOPE + half, (x2 * cost + x1 * sint) * scale)
        put(r0 + QK_HEAD, jnp.zeros((HEAD_PAD - QK_HEAD, tm), F32))


def _qkv(x1, seq, w, tables, tm):
    n = x1.shape[0]
    tiles_per_seq = seq // tm
    tok = lambda width: pl.BlockSpec((tm, width), lambda i: (i, 0))
    tab = pl.BlockSpec((tm, HEAD_PAD), lambda i: (i % tiles_per_seq, 0))
    tab_t = pl.BlockSpec((QK_ROPE // 2, tm), lambda i: (0, i % tiles_per_seq))
    consts = [w["g_mix"], w["w_a"], w["g_q"], w["w_uq_t"], w["g_kv"], w["w_uk"], w["w_uv_t"],
              w["g_qh_col"], w["g_kh"], w["g_kh_swap"]]
    return pl.pallas_call(
        _qkv_body,
        grid=(n // tm,),
        in_specs=([tok(D_MODEL)] + [_const_spec(a.shape) for a in consts]
                  + [tab, tab, tab_t, tab_t]),
        out_specs=[
            pl.BlockSpec((tm // Q_TILE, MLA_HEADS * HEAD_PAD, Q_TILE), lambda i: (i, 0, 0)),
            tok(MLA_HEADS * HEAD_PAD),
            pl.BlockSpec((tm // KV_CHUNK, MLA_HEADS * V_HEAD, KV_CHUNK), lambda i: (i, 0, 0)),
        ],
        out_shape=[jax.ShapeDtypeStruct((n // Q_TILE, MLA_HEADS * HEAD_PAD, Q_TILE), BF16),
                   jax.ShapeDtypeStruct((n, MLA_HEADS * HEAD_PAD), BF16),
                   jax.ShapeDtypeStruct((n // KV_CHUNK, MLA_HEADS * V_HEAD, KV_CHUNK), BF16)],
        compiler_params=_params(1),
        name="qkv",
    )(x1, *consts, *tables)


def _attn_body(qt_ref, k_ref, vt_ref, o_ref, s_ref, acc_ref):
    nq, _, tq = qt_ref.shape
    n_kv = vt_ref.shape[0]
    ones = jnp.ones((SUM_ROWS, KV_CHUNK), BF16)
    m0 = jnp.full((1, tq), -jnp.inf, F32)

    def scores(qi, j, slot):
        rows = pl.ds(pl.multiple_of(j * KV_CHUNK, KV_CHUNK), KV_CHUNK)
        cmax = []
        for hh in range(2):
            st = _dot(k_ref[rows, hh * HEAD_PAD:(hh + 1) * HEAD_PAD],
                      qt_ref[qi, hh * HEAD_PAD:(hh + 1) * HEAD_PAD, :])
            s_ref[hh, slot] = st
            cmax.append(jnp.max(st, axis=0, keepdims=True))
        return tuple(cmax)

    def consume(j, slot, ms, cmax):
        out = []
        for hh in range(2):
            m_new = jnp.maximum(ms[hh], cmax[hh])
            alpha = jnp.exp2(ms[hh] - m_new)
            pt = jnp.exp2(s_ref[hh, slot] - m_new).astype(BF16)
            vt = jnp.concatenate([vt_ref[j, hh * V_HEAD:(hh + 1) * V_HEAD, :], ones], axis=0)
            acc_ref[hh] = alpha * acc_ref[hh] + _dot(vt, pt)
            out.append(m_new)
        return tuple(out)

    def finish(qi):
        heads = [acc_ref[hh, :V_HEAD, :] / acc_ref[hh, V_HEAD:V_HEAD + 1, :] for hh in range(2)]
        rows = pl.ds(pl.multiple_of(qi * tq, tq), tq)
        o_ref[rows, :] = jnp.concatenate(heads, axis=0).T.astype(BF16)
        acc_ref[...] = jnp.zeros_like(acc_ref)

    def q_tile(qi, c_even):
        def pair(t, carry):
            ms, c_even = carry
            c_odd = scores(qi, 2 * t + 1, 1)
            ms = consume(2 * t, 0, ms, c_even)
            c_even = scores(qi, 2 * t + 2, 0)
            ms = consume(2 * t + 1, 1, ms, c_odd)
            return ms, c_even

        ms, c_even = lax.fori_loop(0, n_kv // 2 - 1, pair, ((m0, m0), c_even), unroll=2)
        c_odd = scores(qi, n_kv - 1, 1)
        ms = consume(n_kv - 2, 0, ms, c_even)
        c_next = scores(jnp.minimum(qi + 1, nq - 1), 0, 0)
        consume(n_kv - 1, 1, ms, c_odd)
        finish(qi)
        return c_next

    acc_ref[...] = jnp.zeros_like(acc_ref)
    lax.fori_loop(0, nq, q_tile, scores(0, 0, 0))


def _attn(qt, k, vt, batch, seq):
    n = k.shape[0]
    nq = seq // Q_TILE
    n_kv = seq // KV_CHUNK
    return pl.pallas_call(
        _attn_body,
        grid=(batch, MLA_HEADS // 2),
        in_specs=[
            pl.BlockSpec((nq, 2 * HEAD_PAD, Q_TILE), lambda b, hp: (b, hp, 0)),
            pl.BlockSpec((seq, 2 * HEAD_PAD), lambda b, hp: (b, hp)),
            pl.BlockSpec((n_kv, 2 * V_HEAD, KV_CHUNK), lambda b, hp: (b, hp, 0)),
        ],
        out_specs=pl.BlockSpec((seq, 2 * V_HEAD), lambda b, hp: (b, hp)),
        out_shape=jax.ShapeDtypeStruct((n, MLA_HEADS * V_HEAD), BF16),
        scratch_shapes=[pltpu.VMEM((2, 2, KV_CHUNK, Q_TILE), F32),
                        pltpu.VMEM((2, V_HEAD + SUM_ROWS, Q_TILE), F32)],
        compiler_params=_params(2),
        name="attn",
    )(qt, k, vt)


def _memkv_body(mem_ref, g_ref, w_ref, gk_ref, k_ref, v_ref):
    m = _rms(mem_ref[...], g_ref[...]).astype(BF16)
    kv = _dot(m, w_ref[...])
    width = XA_HEADS * XA_HEAD
    for hd in range(XA_HEADS):
        sl = slice(hd * XA_HEAD, (hd + 1) * XA_HEAD)
        k_ref[:, sl] = _rms(kv[:, sl], gk_ref[...]).astype(BF16)
    v_ref[...] = kv[:, width:].astype(BF16)


def _memkv(mem, w):
    n = mem.shape[0]
    width = XA_HEADS * XA_HEAD
    blk = lambda cols: pl.BlockSpec((N_MEM, cols), lambda b: (b, 0))
    consts = [w["g_mem"], w["w_mem_kv"], w["g_xk"]]
    return pl.pallas_call(
        _memkv_body,
        grid=(n // N_MEM,),
        in_specs=[blk(D_MODEL)] + [_const_spec(a.shape) for a in consts],
        out_specs=[blk(width), blk(width)],
        out_shape=[jax.ShapeDtypeStruct((n, width), BF16)] * 2,
        compiler_params=_params(1),
        name="memkv",
    )(mem, *consts)


def _mix_body(x_ref, xprev_ref, xnext_ref, o_ref, mk_ref, mv_ref, gmix_ref, wcb_ref, wcc_ref,
              wcx_ref, wxq_ref, wg_ref, convw_ref, gxq_ref, womla_ref, woconv_ref, womem_ref,
              wout_ref, y_ref, u_ref, *, tiles_per_seq):
    i = pl.program_id(0)
    tm = x_ref.shape[0]
    x = x_ref[...]
    h = _rms(x, gmix_ref[...]).astype(BF16)

    def gated(hh):
        return _dot(hh, wcc_ref[...]) * _dot(hh, wcx_ref[...])

    first = (i % tiles_per_seq) == 0
    last = (i % tiles_per_seq) == tiles_per_seq - 1
    h_prev = _rms(xprev_ref[...], gmix_ref[...]).astype(BF16)
    h_next = _rms(xnext_ref[...], gmix_ref[...]).astype(BF16)
    u_ref[0:8, :] = jnp.where(first, 0.0, gated(h_prev))
    u_ref[8:8 + tm, :] = gated(h)
    u_ref[8 + tm:16 + tm, :] = jnp.where(last, 0.0, gated(h_next))
    cw = convw_ref[...]
    conv = (u_ref[7:7 + tm, :] * cw[0:1, :] + u_ref[8:8 + tm, :] * cw[1:2, :]
            + u_ref[9:9 + tm, :] * cw[2:3, :])
    y_conv = _dot((_dot(h, wcb_ref[...]) * conv).astype(BF16), woconv_ref[...])

    xq = _dot(h, wxq_ref[...])
    heads = []
    for hd in range(XA_HEADS):
        sl = slice(hd * XA_HEAD, (hd + 1) * XA_HEAD)
        qh = (_rms(xq[:, sl], gxq_ref[...]) * (XA_HEAD ** -0.5)).astype(BF16)
        s = _dot_nt(qh, mk_ref[:, sl])
        p = jnp.exp(s - jnp.max(s, axis=-1, keepdims=True))
        l = jnp.sum(p, axis=-1, keepdims=True)
        heads.append((_dot(p.astype(BF16), mv_ref[:, sl]) / l).astype(BF16))
    y_mem = _dot(jnp.concatenate(heads, axis=-1), womem_ref[...])

    y_mla = _dot(o_ref[...], womla_ref[...])

    gate = lambda b: jax.nn.sigmoid(_dot(h, wg_ref[b]))
    merged = gate(0) * y_mla + gate(1) * y_conv + gate(2) * y_mem
    y_ref[...] = x + _dot(merged.astype(BF16), wout_ref[...])


def _mix(x1, o, mk, mv, seq, w, tm):
    n = x1.shape[0]
    tiles_per_seq = seq // tm
    halo = tm // 8
    n_halo = n // 8
    tok = lambda width: pl.BlockSpec((tm, width), lambda i: (i, 0))
    mem = pl.BlockSpec((N_MEM, XA_HEADS * XA_HEAD), lambda i: (i // tiles_per_seq, 0))
    consts = [w["g_mix"], w["w_cb"], w["w_cc"], w["w_cx"], w["w_xq"], w["w_gate"], w["conv_w"],
              w["g_xq"], w["w_o_mla"], w["w_o_conv"], w["w_o_mem"], w["w_out"]]
    return pl.pallas_call(
        functools.partial(_mix_body, tiles_per_seq=tiles_per_seq),
        grid=(n // tm,),
        in_specs=[
            tok(D_MODEL),
            pl.BlockSpec((8, D_MODEL), lambda i: (jnp.maximum(i * halo - 1, 0), 0)),
            pl.BlockSpec((8, D_MODEL), lambda i: (jnp.minimum((i + 1) * halo, n_halo - 1), 0)),
            tok(MLA_HEADS * V_HEAD), mem, mem,
        ] + [_const_spec(a.shape) for a in consts],
        out_specs=tok(D_MODEL),
        out_shape=jax.ShapeDtypeStruct((n, D_MODEL), F32),
        scratch_shapes=[pltpu.VMEM((tm + 16, CONV_WIDTH), F32)],
        compiler_params=_params(1),
        name="mix",
    )(x1, x1, x1, o, mk, mv, *consts)


def _pad_heads(w, heads, width):
    k = w.shape[0]
    w = w.reshape(k, heads, width)
    return jnp.pad(w, ((0, 0), (0, 0), (0, HEAD_PAD - width))).reshape(k, heads * HEAD_PAD)


def _ffn_weights(norm, w_gu, w_down):
    chunks = lambda w: w.reshape(D_MODEL, N_FF_CHUNKS, FF_CHUNK).transpose(1, 0, 2).astype(BF16)
    return (norm.reshape(1, D_MODEL), chunks(w_gu[:, :D_FF]), chunks(w_gu[:, D_FF:]),
            w_down.reshape(N_FF_CHUNKS, FF_CHUNK, D_MODEL).astype(BF16))


def _rope_tables(seq):
    half = QK_ROPE // 2
    inv_freq = ROPE_BASE ** (-jnp.arange(half, dtype=F32) / half)
    ang = jnp.arange(seq, dtype=jnp.int32).astype(F32)[:, None] * inv_freq[None, :]
    cos, sin = jnp.cos(ang), jnp.sin(ang)
    zeros = lambda width: jnp.zeros((seq, width), F32)
    tail = HEAD_PAD - QK_HEAD
    cos_t = jnp.concatenate([jnp.ones((seq, QK_NOPE), F32), cos, cos, zeros(tail)], axis=1)
    sin_t = jnp.concatenate([zeros(QK_NOPE), -sin, sin, zeros(tail)], axis=1)
    return cos_t, sin_t, cos.T, sin.T


def _trunk(x, mem, w, ffn1, ffn2, tables):
    batch, seq, _ = x.shape
    x = x.reshape(batch * seq, D_MODEL)
    x1 = _ffn(x, *ffn1, tm=512)
    qt, k, vt = _qkv(x1, seq, w, tables, tm=512)
    o = _attn(qt, k, vt, batch, seq)
    mk, mv = _memkv(mem.reshape(batch * N_MEM, D_MODEL), w)
    x2 = _mix(x1, o, mk, mv, seq, w, tm=256)
    y = _ffn(x2, *ffn2, tm=512)
    return y.reshape(batch, seq, D_MODEL)


def kernel(x_prompt, x_sample, mem_prompt, mem_sample, ffn1_norm, ffn1_w_gu, ffn1_w_down, mix_norm, w_in, q_lora_norm, w_uq, kv_lora_norm, w_uk, w_uv, mla_q_norm, mla_k_norm, w_o_mla, conv_w, w_o_conv, mem_norm, w_mem_kv, xa_q_norm, xa_k_norm, w_o_mem, w_out, ffn2_norm, ffn2_w_gu, ffn2_w_down):
    ffn1 = _ffn_weights(ffn1_norm[0], ffn1_w_gu[0], ffn1_w_down[0])
    ffn2 = _ffn_weights(ffn2_norm[0], ffn2_w_gu[0], ffn2_w_down[0])

    w_in0 = w_in[0]
    o_cq, o_ckv, o_kr = 0, Q_LORA, Q_LORA + KV_LORA
    o_cb = o_kr + QK_ROPE
    o_cc, o_cx = o_cb + CONV_WIDTH, o_cb + 2 * CONV_WIDTH
    o_xq = o_cb + 3 * CONV_WIDTH
    o_g = o_xq + XA_HEADS * XA_HEAD
    cols = lambda start, width: w_in0[:, start:start + width]
    half = QK_ROPE // 2
    swap_halves = lambda a: jnp.concatenate([a[..., half:], a[..., :half]], axis=-1)
    rope_lanes = lambda a: jnp.pad(a, ((0, 0), (QK_NOPE, HEAD_PAD - QK_HEAD)))
    kr_pad = rope_lanes(cols(o_kr, QK_ROPE))
    kr_swap_pad = rope_lanes(swap_halves(cols(o_kr, QK_ROPE)))
    gk_swap = rope_lanes(swap_halves(mla_k_norm[0][QK_NOPE:]).reshape(1, QK_ROPE))
    row = lambda g: g.reshape(1, -1)
    pad_gain = lambda g: jnp.pad(g, (0, HEAD_PAD - QK_HEAD))
    w = {
        "g_mix": row(mix_norm[0]),
        "w_a": jnp.concatenate([cols(o_cq, Q_LORA), cols(o_ckv, KV_LORA), kr_pad, kr_swap_pad],
                               axis=1).astype(BF16),
        "g_q": row(q_lora_norm[0]),
        "w_uq_t": _pad_heads(w_uq[0], MLA_HEADS, QK_HEAD).T.astype(BF16),
        "g_kv": row(kv_lora_norm[0]),
        "w_uk": _pad_heads(w_uk[0], MLA_HEADS, QK_NOPE).astype(BF16),
        "w_uv_t": w_uv[0].T.astype(BF16),
        "g_qh_col": pad_gain(mla_q_norm[0]).reshape(HEAD_PAD, 1),
        "g_kh": pad_gain(mla_k_norm[0]).reshape(1, HEAD_PAD),
        "g_kh_swap": gk_swap,
        "w_cb": cols(o_cb, CONV_WIDTH).astype(BF16),
        "w_cc": cols(o_cc, CONV_WIDTH).astype(BF16),
        "w_cx": cols(o_cx, CONV_WIDTH).astype(BF16),
        "w_xq": cols(o_xq, XA_HEADS * XA_HEAD).astype(BF16),
        "w_gate": cols(o_g, 3 * D_MODEL).reshape(D_MODEL, 3, D_MODEL).transpose(1, 0, 2).astype(BF16),
        "conv_w": conv_w[0],
        "g_xq": row(xa_q_norm[0]),
        "w_o_mla": w_o_mla[0].astype(BF16),
        "w_o_conv": w_o_conv[0].astype(BF16),
        "w_o_mem": w_o_mem[0].astype(BF16),
        "w_out": w_out[0].astype(BF16),
        "g_mem": row(mem_norm[0]),
        "w_mem_kv": w_mem_kv[0].astype(BF16),
        "g_xk": row(xa_k_norm[0]),
    }
    tables = _rope_tables(max(x_prompt.shape[1], x_sample.shape[1]))
    y_prompt = _trunk(x_prompt, mem_prompt, w, ffn1, ffn2, tables)
    y_sample = _trunk(x_sample, mem_sample, w, ffn1, ffn2, tables)
    return (y_prompt, y_sample)
```

```python
import functools

import jax
import jax.numpy as jnp
from jax import lax
from jax.experimental import pallas as pl
from jax.experimental.pallas import tpu as pltpu

D_MODEL = 1024
N_MEM = 256
MLA_HEADS = 8
QK_NOPE = 64
QK_ROPE = 32
QK_HEAD = QK_NOPE + QK_ROPE
V_HEAD = 64
Q_LORA = 384
KV_LORA = 256
CONV_WIDTH = 512
XA_HEADS = 4
XA_HEAD = 128
D_FF = 2816
ROPE_BASE = 10000.0
EPS = 1e-6
LOG2_E = 1.4426950408889634

LANES = 128
HEAD_PAD = LANES
FF_CHUNK = 256
N_FF_CHUNKS = D_FF // FF_CHUNK
KV_CHUNK = 512
Q_TILE = 512
SUM_ROWS = 16
VMEM_LIMIT = 56 * 1024 * 1024

BF16 = jnp.bfloat16
F32 = jnp.float32


def _const_spec(shape):
    zeros = (0,) * len(shape)
    return pl.BlockSpec(shape, lambda *_: zeros, pipeline_mode=pl.Buffered(1))


def _params(n_axes):
    return pltpu.CompilerParams(
        dimension_semantics=("arbitrary",) * n_axes,
        vmem_limit_bytes=VMEM_LIMIT)


def _rms(x, gain, n=None):
    n = x.shape[-1] if n is None else n
    inv = lax.rsqrt(jnp.sum(x * x, axis=-1, keepdims=True) * (1.0 / n) + EPS)
    return (x * inv) * gain


def _dot(a, b):
    return jnp.dot(a, b, preferred_element_type=F32)


def _dot_nt(a, b):
    return lax.dot_general(a, b, (((1,), (1,)), ((), ())), preferred_element_type=F32)


def _ffn_body(x_ref, g_ref, wgu_ref, wd_ref, o_ref, xn_ref, acc_ref):
    xn_ref[...] = _rms(x_ref[...], g_ref[...]).astype(BF16)
    acc_ref[...] = jnp.zeros_like(acc_ref)
    for c in range(N_FF_CHUNKS):
        cols = slice(c * FF_CHUNK, (c + 1) * FF_CHUNK)
        up_cols = slice(D_FF + c * FF_CHUNK, D_FF + (c + 1) * FF_CHUNK)
        xn = xn_ref[...]
        g = _dot(xn, wgu_ref[:, cols])
        u = _dot(xn, wgu_ref[:, up_cols])
        a = (g * jax.nn.sigmoid(g) * u).astype(BF16)
        acc_ref[...] += _dot(a, wd_ref[cols, :])
    o_ref[...] = x_ref[...] + 0.5 * acc_ref[...]


def _ffn(x, gain, wgu, wd, tm):
    n = x.shape[0]
    return pl.pallas_call(
        _ffn_body,
        grid=(n // tm,),
        in_specs=[
            pl.BlockSpec((tm, D_MODEL), lambda i: (i, 0)),
            _const_spec((1, D_MODEL)),
            _const_spec(wgu.shape),
            _const_spec(wd.shape),
        ],
        out_specs=pl.BlockSpec((tm, D_MODEL), lambda i: (i, 0)),
        out_shape=jax.ShapeDtypeStruct((n, D_MODEL), F32),
        scratch_shapes=[pltpu.VMEM((tm, D_MODEL), BF16), pltpu.VMEM((tm, D_MODEL), F32)],
        compiler_params=_params(1),
        name="ffn",
    )(x, gain, wgu, wd)


def _qkv_body(x_ref, gmix_ref, wa_ref, gq_ref, wuqt_ref, gkv_ref, wuk_ref, wuvt_ref,
              gqh_ref, gkh_ref, gkhs_ref, cos_ref, sin_ref, cost_ref, sint_ref,
              qt_ref, k_ref, vt_ref):
    tm = x_ref.shape[0]
    half = QK_ROPE // 2
    h = _rms(x_ref[...], gmix_ref[...]).astype(BF16)
    c = _dot(h, wa_ref[...])
    cq = _rms(c[:, :Q_LORA], gq_ref[...]).astype(BF16)
    ckv = _rms(c[:, Q_LORA:Q_LORA + KV_LORA], gkv_ref[...]).astype(BF16)
    o_kr = Q_LORA + KV_LORA
    kr = c[:, o_kr:o_kr + HEAD_PAD]
    kr_swap = c[:, o_kr + HEAD_PAD:]

    vt = _dot_nt(wuvt_ref[...], ckv).astype(BF16)
    for ch in range(tm // KV_CHUNK):
        vt_ref[ch] = vt[:, ch * KV_CHUNK:(ch + 1) * KV_CHUNK]

    kn = _dot(ckv, wuk_ref[...])
    cos, sin = cos_ref[...], sin_ref[...]
    for hd in range(MLA_HEADS):
        sl = slice(hd * HEAD_PAD, (hd + 1) * HEAD_PAD)
        kh = kn[:, sl] + kr
        inv = lax.rsqrt(jnp.sum(kh * kh, axis=-1, keepdims=True) * (1.0 / QK_HEAD) + EPS)
        rot = ((kh * inv) * gkh_ref[...]) * cos + ((kr_swap * inv) * gkhs_ref[...]) * sin
        k_ref[:, sl] = rot.astype(BF16)

    qt = _dot_nt(wuqt_ref[...], cq)
    cost, sint = cost_ref[...], sint_ref[...]
    scale = QK_HEAD ** -0.5 * LOG2_E

    def put(row, rows_f32):
        for t in range(tm // Q_TILE):
            qt_ref[t, row:row + rows_f32.shape[0], :] = (
                rows_f32[:, t * Q_TILE:(t + 1) * Q_TILE].astype(BF16))

    for hd in range(MLA_HEADS):
        r0 = hd * HEAD_PAD
        blk = qt[r0:r0 + HEAD_PAD, :]
        inv = lax.rsqrt(jnp.sum(blk * blk, axis=0, keepdims=True) * (1.0 / QK_HEAD) + EPS)
        qn = (blk * inv) * gqh_ref[...]
        x1, x2 = qn[QK_NOPE:QK_NOPE + half], qn[QK_NOPE + half:QK_HEAD]
        put(r0, qn[:QK_NOPE] * scale)
        put(r0 + QK_NOPE, (x1 * cost - x2 * sint) * scale)
        put(r0 + QK_NOPE + half, (x2 * cost + x1 * sint) * scale)
        put(r0 + QK_HEAD, jnp.zeros((HEAD_PAD - QK_HEAD, tm), F32))


def _qkv(x1, seq, w, tables, tm):
    n = x1.shape[0]
    tiles_per_seq = seq // tm
    tok = lambda width: pl.BlockSpec((tm, width), lambda i: (i, 0))
    tab = pl.BlockSpec((tm, HEAD_PAD), lambda i: (i % tiles_per_seq, 0))
    tab_t = pl.BlockSpec((QK_ROPE // 2, tm), lambda i: (0, i % tiles_per_seq))
    consts = [w["g_mix"], w["w_a"], w["g_q"], w["w_uq_t"], w["g_kv"], w["w_uk"], w["w_uv_t"],
              w["g_qh_col"], w["g_kh"], w["g_kh_swap"]]
    return pl.pallas_call(
        _qkv_body,
        grid=(n // tm,),
        in_specs=([tok(D_MODEL)] + [_const_spec(a.shape) for a in consts]
                  + [tab, tab, tab_t, tab_t]),
        out_specs=[
            pl.BlockSpec((tm // Q_TILE, MLA_HEADS * HEAD_PAD, Q_TILE), lambda i: (i, 0, 0)),
            tok(MLA_HEADS * HEAD_PAD),
            pl.BlockSpec((tm // KV_CHUNK, MLA_HEADS * V_HEAD, KV_CHUNK), lambda i: (i, 0, 0)),
        ],
        out_shape=[jax.ShapeDtypeStruct((n // Q_TILE, MLA_HEADS * HEAD_PAD, Q_TILE), BF16),
                   jax.ShapeDtypeStruct((n, MLA_HEADS * HEAD_PAD), BF16),
                   jax.ShapeDtypeStruct((n // KV_CHUNK, MLA_HEADS * V_HEAD, KV_CHUNK), BF16)],
        compiler_params=_params(1),
        name="qkv",
    )(x1, *consts, *tables)


def _attn_body(qt_ref, k_ref, vt_ref, o_ref, s_ref, acc_ref):
    nq, _, tq = qt_ref.shape
    n_kv = vt_ref.shape[0]
    ones = jnp.ones((SUM_ROWS, KV_CHUNK), BF16)
    m0 = jnp.full((1, tq), -jnp.inf, F32)

    def scores(qi, j, slot):
        rows = pl.ds(pl.multiple_of(j * KV_CHUNK, KV_CHUNK), KV_CHUNK)
        cmax = []
        for hh in range(2):
            st = _dot(k_ref[rows, hh * HEAD_PAD:(hh + 1) * HEAD_PAD],
                      qt_ref[qi, hh * HEAD_PAD:(hh + 1) * HEAD_PAD, :])
            s_ref[hh, slot] = st
            cmax.append(jnp.max(st, axis=0, keepdims=True))
        return tuple(cmax)

    def consume(j, slot, ms, cmax):
        out = []
        for hh in range(2):
            m_new = jnp.maximum(ms[hh], cmax[hh])
            alpha = jnp.exp2(ms[hh] - m_new)
            pt = jnp.exp2(s_ref[hh, slot] - m_new).astype(BF16)
            vt = jnp.concatenate([vt_ref[j, hh * V_HEAD:(hh + 1) * V_HEAD, :], ones], axis=0)
            acc_ref[hh] = alpha * acc_ref[hh] + _dot(vt, pt)
            out.append(m_new)
        return tuple(out)

    def finish(qi):
        heads = [acc_ref[hh, :V_HEAD, :] / acc_ref[hh, V_HEAD:V_HEAD + 1, :] for hh in range(2)]
        rows = pl.ds(pl.multiple_of(qi * tq, tq), tq)
        o_ref[rows, :] = jnp.concatenate(heads, axis=0).T.astype(BF16)
        acc_ref[...] = jnp.zeros_like(acc_ref)

    def q_tile(qi, c_even):
        def pair(t, carry):
            ms, c_even = carry
            c_odd = scores(qi, 2 * t + 1, 1)
            ms = consume(2 * t, 0, ms, c_even)
            c_even = scores(qi, 2 * t + 2, 0)
            ms = consume(2 * t + 1, 1, ms, c_odd)
            return ms, c_even

        ms, c_even = lax.fori_loop(0, n_kv // 2 - 1, pair, ((m0, m0), c_even), unroll=2)
        c_odd = scores(qi, n_kv - 1, 1)
        ms = consume(n_kv - 2, 0, ms, c_even)
        c_next = scores(jnp.minimum(qi + 1, nq - 1), 0, 0)
        consume(n_kv - 1, 1, ms, c_odd)
        finish(qi)
        return c_next

    acc_ref[...] = jnp.zeros_like(acc_ref)
    lax.fori_loop(0, nq, q_tile, scores(0, 0, 0), unroll=2)


def _attn(qt, k, vt, batch, seq):
    n = k.shape[0]
    nq = seq // Q_TILE
    n_kv = seq // KV_CHUNK
    return pl.pallas_call(
        _attn_body,
        grid=(batch, MLA_HEADS // 2),
        in_specs=[
            pl.BlockSpec((nq, 2 * HEAD_PAD, Q_TILE), lambda b, hp: (b, hp, 0)),
            pl.BlockSpec((seq, 2 * HEAD_PAD), lambda b, hp: (b, hp)),
            pl.BlockSpec((n_kv, 2 * V_HEAD, KV_CHUNK), lambda b, hp: (b, hp, 0)),
        ],
        out_specs=pl.BlockSpec((seq, 2 * V_HEAD), lambda b, hp: (b, hp)),
        out_shape=jax.ShapeDtypeStruct((n, MLA_HEADS * V_HEAD), BF16),
        scratch_shapes=[pltpu.VMEM((2, 2, KV_CHUNK, Q_TILE), F32),
                        pltpu.VMEM((2, V_HEAD + SUM_ROWS, Q_TILE), F32)],
        compiler_params=_params(2),
        name="attn",
    )(qt, k, vt)


def _memkv_body(mem_ref, g_ref, w_ref, gk_ref, k_ref, v_ref):
    m = _rms(mem_ref[...], g_ref[...]).astype(BF16)
    kv = _dot(m, w_ref[...])
    width = XA_HEADS * XA_HEAD
    for hd in range(XA_HEADS):
        sl = slice(hd * XA_HEAD, (hd + 1) * XA_HEAD)
        k_ref[:, sl] = _rms(kv[:, sl], gk_ref[...]).astype(BF16)
    v_ref[...] = kv[:, width:].astype(BF16)


def _memkv(mem, w):
    n = mem.shape[0]
    width = XA_HEADS * XA_HEAD
    blk = lambda cols: pl.BlockSpec((N_MEM, cols), lambda b: (b, 0))
    consts = [w["g_mem"], w["w_mem_kv"], w["g_xk"]]
    return pl.pallas_call(
        _memkv_body,
        grid=(n // N_MEM,),
        in_specs=[blk(D_MODEL)] + [_const_spec(a.shape) for a in consts],
        out_specs=[blk(width), blk(width)],
        out_shape=[jax.ShapeDtypeStruct((n, width), BF16)] * 2,
        compiler_params=_params(1),
        name="memkv",
    )(mem, *consts)


def _mix_body(x_ref, xprev_ref, xnext_ref, o_ref, mk_ref, mv_ref, gmix_ref, win_ref,
              convw_ref, gxq_ref, womla_ref, woconv_ref, womem_ref,
              wout_ref, y_ref, u_ref, *, tiles_per_seq):
    i = pl.program_id(0)
    tm = x_ref.shape[0]
    x = x_ref[...]
    h = _rms(x, gmix_ref[...]).astype(BF16)

    o_cc, o_cx, o_xq = CONV_WIDTH, 2 * CONV_WIDTH, 3 * CONV_WIDTH
    o_gate = o_xq + XA_HEADS * XA_HEAD
    proj = lambda hh, start, width: _dot(hh, win_ref[:, start:start + width])

    def gated(hh):
        return proj(hh, o_cc, CONV_WIDTH) * proj(hh, o_cx, CONV_WIDTH)

    first = (i % tiles_per_seq) == 0
    last = (i % tiles_per_seq) == tiles_per_seq - 1
    h_prev = _rms(xprev_ref[...], gmix_ref[...]).astype(BF16)
    h_next = _rms(xnext_ref[...], gmix_ref[...]).astype(BF16)
    u_ref[0:8, :] = jnp.where(first, 0.0, gated(h_prev))
    u_ref[8:8 + tm, :] = gated(h)
    u_ref[8 + tm:16 + tm, :] = jnp.where(last, 0.0, gated(h_next))
    cw = convw_ref[...]
    conv = (u_ref[7:7 + tm, :] * cw[0:1, :] + u_ref[8:8 + tm, :] * cw[1:2, :]
            + u_ref[9:9 + tm, :] * cw[2:3, :])
    y_conv = _dot((proj(h, 0, CONV_WIDTH) * conv).astype(BF16), woconv_ref[...])

    xq = proj(h, o_xq, XA_HEADS * XA_HEAD)
    heads = []
    for hd in range(XA_HEADS):
        sl = slice(hd * XA_HEAD, (hd + 1) * XA_HEAD)
        qh = (_rms(xq[:, sl], gxq_ref[...]) * (XA_HEAD ** -0.5)).astype(BF16)
        s = _dot_nt(qh, mk_ref[:, sl])
        p = jnp.exp(s - jnp.max(s, axis=-1, keepdims=True))
        l = jnp.sum(p, axis=-1, keepdims=True)
        heads.append((_dot(p.astype(BF16), mv_ref[:, sl]) / l).astype(BF16))
    y_mem = _dot(jnp.concatenate(heads, axis=-1), womem_ref[...])

    y_mla = _dot(o_ref[...], womla_ref[...])

    gate = lambda b: jax.nn.sigmoid(proj(h, o_gate + b * D_MODEL, D_MODEL))
    merged = gate(0) * y_mla + gate(1) * y_conv + gate(2) * y_mem
    y_ref[...] = x + _dot(merged.astype(BF16), wout_ref[...])


def _mix(x1, o, mk, mv, seq, w, tm):
    n = x1.shape[0]
    tiles_per_seq = seq // tm
    halo = tm // 8
    n_halo = n // 8
    tok = lambda width: pl.BlockSpec((tm, width), lambda i: (i, 0))
    mem = pl.BlockSpec((N_MEM, XA_HEADS * XA_HEAD), lambda i: (i // tiles_per_seq, 0))
    consts = [w["g_mix"], w["w_in_mix"], w["conv_w"],
              w["g_xq"], w["w_o_mla"], w["w_o_conv"], w["w_o_mem"], w["w_out"]]
    return pl.pallas_call(
        functools.partial(_mix_body, tiles_per_seq=tiles_per_seq),
        grid=(n // tm,),
        in_specs=[
            tok(D_MODEL),
            pl.BlockSpec((8, D_MODEL), lambda i: (jnp.maximum(i * halo - 1, 0), 0)),
            pl.BlockSpec((8, D_MODEL), lambda i: (jnp.minimum((i + 1) * halo, n_halo - 1), 0)),
            tok(MLA_HEADS * V_HEAD), mem, mem,
        ] + [_const_spec(a.shape) for a in consts],
        out_specs=tok(D_MODEL),
        out_shape=jax.ShapeDtypeStruct((n, D_MODEL), F32),
        scratch_shapes=[pltpu.VMEM((tm + 16, CONV_WIDTH), F32)],
        compiler_params=_params(1),
        name="mix",
    )(x1, x1, x1, o, mk, mv, *consts)


def _pad_heads(w, heads, width):
    k = w.shape[0]
    w = w.reshape(k, heads, width)
    return jnp.pad(w, ((0, 0), (0, 0), (0, HEAD_PAD - width))).reshape(k, heads * HEAD_PAD)


def _ffn_weights(norm, w_gu, w_down):
    return norm.reshape(1, D_MODEL), w_gu.astype(BF16), w_down.astype(BF16)


def _rope_tables(seq):
    half = QK_ROPE // 2
    inv_freq = ROPE_BASE ** (-jnp.arange(half, dtype=F32) / half)
    ang = jnp.arange(seq, dtype=jnp.int32).astype(F32)[:, None] * inv_freq[None, :]
    cos, sin = jnp.cos(ang), jnp.sin(ang)
    zeros = lambda width: jnp.zeros((seq, width), F32)
    tail = HEAD_PAD - QK_HEAD
    cos_t = jnp.concatenate([jnp.ones((seq, QK_NOPE), F32), cos, cos, zeros(tail)], axis=1)
    sin_t = jnp.concatenate([zeros(QK_NOPE), -sin, sin, zeros(tail)], axis=1)
    return cos_t, sin_t, cos.T, sin.T


def _trunk(x, mem, w, ffn1, ffn2, tables):
    batch, seq, _ = x.shape
    x = x.reshape(batch * seq, D_MODEL)
    x1 = _ffn(x, *ffn1, tm=512)
    qt, k, vt = _qkv(x1, seq, w, tables, tm=512)
    o = _attn(qt, k, vt, batch, seq)
    mk, mv = _memkv(mem.reshape(batch * N_MEM, D_MODEL), w)
    x2 = _mix(x1, o, mk, mv, seq, w, tm=512)
    y = _ffn(x2, *ffn2, tm=512)
    return y.reshape(batch, seq, D_MODEL)


def kernel(x_prompt, x_sample, mem_prompt, mem_sample, ffn1_norm, ffn1_w_gu, ffn1_w_down, mix_norm, w_in, q_lora_norm, w_uq, kv_lora_norm, w_uk, w_uv, mla_q_norm, mla_k_norm, w_o_mla, conv_w, w_o_conv, mem_norm, w_mem_kv, xa_q_norm, xa_k_norm, w_o_mem, w_out, ffn2_norm, ffn2_w_gu, ffn2_w_down):
    ffn1 = _ffn_weights(ffn1_norm[0], ffn1_w_gu[0], ffn1_w_down[0])
    ffn2 = _ffn_weights(ffn2_norm[0], ffn2_w_gu[0], ffn2_w_down[0])

    w_in0 = w_in[0]
    o_cq, o_ckv, o_kr = 0, Q_LORA, Q_LORA + KV_LORA
    o_cb = o_kr + QK_ROPE
    o_cc, o_cx = o_cb + CONV_WIDTH, o_cb + 2 * CONV_WIDTH
    o_xq = o_cb + 3 * CONV_WIDTH
    o_g = o_xq + XA_HEADS * XA_HEAD
    cols = lambda start, width: w_in0[:, start:start + width]
    half = QK_ROPE // 2
    swap_halves = lambda a: jnp.concatenate([a[..., half:], a[..., :half]], axis=-1)
    rope_lanes = lambda a: jnp.pad(a, ((0, 0), (QK_NOPE, HEAD_PAD - QK_HEAD)))
    kr_pad = rope_lanes(cols(o_kr, QK_ROPE))
    kr_swap_pad = rope_lanes(swap_halves(cols(o_kr, QK_ROPE)))
    gk_swap = rope_lanes(swap_halves(mla_k_norm[0][QK_NOPE:]).reshape(1, QK_ROPE))
    row = lambda g: g.reshape(1, -1)
    pad_gain = lambda g: jnp.pad(g, (0, HEAD_PAD - QK_HEAD))
    w = {
        "g_mix": row(mix_norm[0]),
        "w_a": jnp.concatenate([cols(o_cq, Q_LORA), cols(o_ckv, KV_LORA), kr_pad, kr_swap_pad],
                               axis=1).astype(BF16),
        "g_q": row(q_lora_norm[0]),
        "w_uq_t": _pad_heads(w_uq[0], MLA_HEADS, QK_HEAD).T.astype(BF16),
        "g_kv": row(kv_lora_norm[0]),
        "w_uk": _pad_heads(w_uk[0], MLA_HEADS, QK_NOPE).astype(BF16),
        "w_uv_t": w_uv[0].T.astype(BF16),
        "g_qh_col": pad_gain(mla_q_norm[0]).reshape(HEAD_PAD, 1),
        "g_kh": pad_gain(mla_k_norm[0]).reshape(1, HEAD_PAD),
        "g_kh_swap": gk_swap,
        "w_in_mix": w_in0[:, o_cb:].astype(BF16),
        "conv_w": conv_w[0],
        "g_xq": row(xa_q_norm[0]),
        "w_o_mla": w_o_mla[0].astype(BF16),
        "w_o_conv": w_o_conv[0].astype(BF16),
        "w_o_mem": w_o_mem[0].astype(BF16),
        "w_out": w_out[0].astype(BF16),
        "g_mem": row(mem_norm[0]),
        "w_mem_kv": w_mem_kv[0].astype(BF16),
        "g_xk": row(xa_k_norm[0]),
    }
    tables = _rope_tables(max(x_prompt.shape[1], x_sample.shape[1]))
    y_prompt = _trunk(x_prompt, mem_prompt, w, ffn1, ffn2, tables)
    y_sample = _trunk(x_sample, mem_sample, w, ffn1, ffn2, tables)
    return (y_prompt, y_sample)
```

```python
import functools

import jax
import jax.numpy as jnp
from jax import lax
from jax.experimental import pallas as pl
from jax.experimental.pallas import tpu as pltpu

D_MODEL = 1024
N_MEM = 256
MLA_HEADS = 8
QK_NOPE = 64
QK_ROPE = 32
QK_HEAD = QK_NOPE + QK_ROPE
V_HEAD = 64
Q_LORA = 384
KV_LORA = 256
CONV_WIDTH = 512
XA_HEADS = 4
XA_HEAD = 128
D_FF = 2816
ROPE_BASE = 10000.0
EPS = 1e-6
LOG2_E = 1.4426950408889634

LANES = 128
HEAD_PAD = LANES
FF_CHUNK = 256
N_FF_CHUNKS = D_FF // FF_CHUNK
KV_CHUNK = 512
Q_TILE = 512
HEAD_PASSES = ((0,), (1,))
SUM_ROWS = 16
VMEM_LIMIT = 56 * 1024 * 1024

BF16 = jnp.bfloat16
F32 = jnp.float32


def _const_spec(shape):
    zeros = (0,) * len(shape)
    return pl.BlockSpec(shape, lambda *_: zeros, pipeline_mode=pl.Buffered(1))


def _params(n_axes):
    return pltpu.CompilerParams(
        dimension_semantics=("arbitrary",) * n_axes,
        vmem_limit_bytes=VMEM_LIMIT)


def _rms(x, gain, n=None):
    n = x.shape[-1] if n is None else n
    inv = lax.rsqrt(jnp.sum(x * x, axis=-1, keepdims=True) * (1.0 / n) + EPS)
    return (x * inv) * gain


def _dot(a, b):
    return jnp.dot(a, b, preferred_element_type=F32)


def _dot_nt(a, b):
    return lax.dot_general(a, b, (((1,), (1,)), ((), ())), preferred_element_type=F32)


def _ffn_body(x_ref, g_ref, wgu_ref, wd_ref, o_ref, xn_ref, acc_ref):
    xn_ref[...] = _rms(x_ref[...], g_ref[...]).astype(BF16)
    acc_ref[...] = jnp.zeros_like(acc_ref)
    for c in range(N_FF_CHUNKS):
        cols = slice(c * FF_CHUNK, (c + 1) * FF_CHUNK)
        up_cols = slice(D_FF + c * FF_CHUNK, D_FF + (c + 1) * FF_CHUNK)
        xn = xn_ref[...]
        g = _dot(xn, wgu_ref[:, cols])
        u = _dot(xn, wgu_ref[:, up_cols])
        a = (g * jax.nn.sigmoid(g) * u).astype(BF16)
        acc_ref[...] += _dot(a, wd_ref[cols, :])
    o_ref[...] = x_ref[...] + 0.5 * acc_ref[...]


def _ffn(x, gain, wgu, wd, tm):
    n = x.shape[0]
    return pl.pallas_call(
        _ffn_body,
        grid=(n // tm,),
        in_specs=[
            pl.BlockSpec((tm, D_MODEL), lambda i: (i, 0)),
            _const_spec((1, D_MODEL)),
            _const_spec(wgu.shape),
            _const_spec(wd.shape),
        ],
        out_specs=pl.BlockSpec((tm, D_MODEL), lambda i: (i, 0)),
        out_shape=jax.ShapeDtypeStruct((n, D_MODEL), F32),
        scratch_shapes=[pltpu.VMEM((tm, D_MODEL), BF16), pltpu.VMEM((tm, D_MODEL), F32)],
        compiler_params=_params(1),
        name="ffn",
    )(x, gain, wgu, wd)


def _qkv_body(x_ref, gmix_ref, wa_ref, gq_ref, wuqt_ref, gkv_ref, wuk_ref, wuvt_ref,
              gqh_ref, gkh_ref, gkhs_ref, cos_ref, sin_ref, cost_ref, sint_ref,
              qt_ref, k_ref, vt_ref):
    tm = x_ref.shape[0]
    half = QK_ROPE // 2
    h = _rms(x_ref[...], gmix_ref[...]).astype(BF16)
    c = _dot(h, wa_ref[...])
    cq = _rms(c[:, :Q_LORA], gq_ref[...]).astype(BF16)
    ckv = _rms(c[:, Q_LORA:Q_LORA + KV_LORA], gkv_ref[...]).astype(BF16)
    o_kr = Q_LORA + KV_LORA
    kr = c[:, o_kr:o_kr + HEAD_PAD]
    kr_swap = c[:, o_kr + HEAD_PAD:]

    vt = _dot_nt(wuvt_ref[...], ckv).astype(BF16)
    for ch in range(tm // KV_CHUNK):
        vt_ref[ch] = vt[:, ch * KV_CHUNK:(ch + 1) * KV_CHUNK]

    kn = _dot(ckv, wuk_ref[...])
    cos, sin = cos_ref[...], sin_ref[...]
    for hd in range(MLA_HEADS):
        sl = slice(hd * HEAD_PAD, (hd + 1) * HEAD_PAD)
        kh = kn[:, sl] + kr
        inv = lax.rsqrt(jnp.sum(kh * kh, axis=-1, keepdims=True) * (1.0 / QK_HEAD) + EPS)
        rot = ((kh * inv) * gkh_ref[...]) * cos + ((kr_swap * inv) * gkhs_ref[...]) * sin
        k_ref[:, sl] = rot.astype(BF16)

    qt = _dot_nt(wuqt_ref[...], cq)
    cost, sint = cost_ref[...], sint_ref[...]
    scale = QK_HEAD ** -0.5 * LOG2_E

    def put(row, rows_f32):
        for t in range(tm // Q_TILE):
            qt_ref[t, row:row + rows_f32.shape[0], :] = (
                rows_f32[:, t * Q_TILE:(t + 1) * Q_TILE].astype(BF16))

    for hd in range(MLA_HEADS):
        r0 = hd * HEAD_PAD
        blk = qt[r0:r0 + HEAD_PAD, :]
        inv = lax.rsqrt(jnp.sum(blk * blk, axis=0, keepdims=True) * (1.0 / QK_HEAD) + EPS)
        qn = (blk * inv) * gqh_ref[...]
        x1, x2 = qn[QK_NOPE:QK_NOPE + half], qn[QK_NOPE + half:QK_HEAD]
        put(r0, qn[:QK_NOPE] * scale)
        put(r0 + QK_NOPE, (x1 * cost - x2 * sint) * scale)
        put(r0 + QK_NOPE + half, (x2 * cost + x1 * sint) * scale)
        put(r0 + QK_HEAD, jnp.zeros((HEAD_PAD - QK_HEAD, tm), F32))


def _qkv(x1, seq, w, tables, tm):
    n = x1.shape[0]
    tiles_per_seq = seq // tm
    tok = lambda width: pl.BlockSpec((tm, width), lambda i: (i, 0))
    tab = pl.BlockSpec((tm, HEAD_PAD), lambda i: (i % tiles_per_seq, 0))
    tab_t = pl.BlockSpec((QK_ROPE // 2, tm), lambda i: (0, i % tiles_per_seq))
    consts = [w["g_mix"], w["w_a"], w["g_q"], w["w_uq_t"], w["g_kv"], w["w_uk"], w["w_uv_t"],
              w["g_qh_col"], w["g_kh"], w["g_kh_swap"]]
    return pl.pallas_call(
        _qkv_body,
        grid=(n // tm,),
        in_specs=([tok(D_MODEL)] + [_const_spec(a.shape) for a in consts]
                  + [tab, tab, tab_t, tab_t]),
        out_specs=[
            pl.BlockSpec((tm // Q_TILE, MLA_HEADS * HEAD_PAD, Q_TILE), lambda i: (i, 0, 0)),
            tok(MLA_HEADS * HEAD_PAD),
            pl.BlockSpec((tm // KV_CHUNK, MLA_HEADS * V_HEAD, KV_CHUNK), lambda i: (i, 0, 0)),
        ],
        out_shape=[jax.ShapeDtypeStruct((n // Q_TILE, MLA_HEADS * HEAD_PAD, Q_TILE), BF16),
                   jax.ShapeDtypeStruct((n, MLA_HEADS * HEAD_PAD), BF16),
                   jax.ShapeDtypeStruct((n // KV_CHUNK, MLA_HEADS * V_HEAD, KV_CHUNK), BF16)],
        compiler_params=_params(1),
        name="qkv",
    )(x1, *consts, *tables)


def _attn_body(qt_ref, k_ref, vt_ref, o_ref, s_ref, acc_ref, ot_ref):
    nq, _, tq = qt_ref.shape
    n_kv = vt_ref.shape[0]
    ones = jnp.ones((SUM_ROWS, KV_CHUNK), BF16)
    m0 = jnp.full((1, tq), -jnp.inf, F32)

    def run(hs):
        def scores(qi, j, slot):
            rows = pl.ds(pl.multiple_of(j * KV_CHUNK, KV_CHUNK), KV_CHUNK)
            cmax = []
            for hh in hs:
                st = _dot(k_ref[rows, hh * HEAD_PAD:(hh + 1) * HEAD_PAD],
                          qt_ref[qi, hh * HEAD_PAD:(hh + 1) * HEAD_PAD, :])
                s_ref[hh, slot] = st
                cmax.append(jnp.max(st, axis=0, keepdims=True))
            return tuple(cmax)

        def consume(j, slot, ms, cmax):
            out = []
            for n, hh in enumerate(hs):
                m_new = jnp.maximum(ms[n], cmax[n])
                alpha = jnp.exp2(ms[n] - m_new)
                pt = jnp.exp2(s_ref[hh, slot] - m_new).astype(BF16)
                vt = jnp.concatenate([vt_ref[j, hh * V_HEAD:(hh + 1) * V_HEAD, :], ones], axis=0)
                acc_ref[hh] = alpha * acc_ref[hh] + _dot(vt, pt)
                out.append(m_new)
            return tuple(out)

        def finish(qi):
            for hh in hs:
                ot_ref[qi, hh * V_HEAD:(hh + 1) * V_HEAD, :] = (
                    acc_ref[hh, :V_HEAD, :] / acc_ref[hh, V_HEAD:V_HEAD + 1, :])
                acc_ref[hh] = jnp.zeros(acc_ref.shape[1:], F32)

        def q_tile(qi, c_even):
            def pair(t, carry):
                ms, c_even = carry
                c_odd = scores(qi, 2 * t + 1, 1)
                ms = consume(2 * t, 0, ms, c_even)
                c_even = scores(qi, 2 * t + 2, 0)
                ms = consume(2 * t + 1, 1, ms, c_odd)
                return ms, c_even

            ms0 = (m0,) * len(hs)
            ms, c_even = lax.fori_loop(0, n_kv // 2 - 1, pair, (ms0, c_even), unroll=2)
            c_odd = scores(qi, n_kv - 1, 1)
            ms = consume(n_kv - 2, 0, ms, c_even)
            c_next = scores(jnp.minimum(qi + 1, nq - 1), 0, 0)
            consume(n_kv - 1, 1, ms, c_odd)
            finish(qi)
            return c_next

        lax.fori_loop(0, nq, q_tile, scores(0, 0, 0), unroll=2)

    acc_ref[...] = jnp.zeros_like(acc_ref)
    for hs in HEAD_PASSES:
        run(hs)

    def emit(qi, carry):
        rows = pl.ds(pl.multiple_of(qi * tq, tq), tq)
        o_ref[rows, :] = ot_ref[qi].T.astype(BF16)
        return carry

    lax.fori_loop(0, nq, emit, 0)


def _attn(qt, k, vt, batch, seq):
    n = k.shape[0]
    nq = seq // Q_TILE
    n_kv = seq // KV_CHUNK
    return pl.pallas_call(
        _attn_body,
        grid=(batch, MLA_HEADS // 2),
        in_specs=[
            pl.BlockSpec((nq, 2 * HEAD_PAD, Q_TILE), lambda b, hp: (b, hp, 0)),
            pl.BlockSpec((seq, 2 * HEAD_PAD), lambda b, hp: (b, hp)),
            pl.BlockSpec((n_kv, 2 * V_HEAD, KV_CHUNK), lambda b, hp: (b, hp, 0)),
        ],
        out_specs=pl.BlockSpec((seq, 2 * V_HEAD), lambda b, hp: (b, hp)),
        out_shape=jax.ShapeDtypeStruct((n, MLA_HEADS * V_HEAD), BF16),
        scratch_shapes=[pltpu.VMEM((2, 2, KV_CHUNK, Q_TILE), F32),
                        pltpu.VMEM((2, V_HEAD + SUM_ROWS, Q_TILE), F32),
                        pltpu.VMEM((nq, 2 * V_HEAD, Q_TILE), F32)],
        compiler_params=_params(2),
        name="attn",
    )(qt, k, vt)


def _memkv_body(mem_ref, g_ref, w_ref, gk_ref, k_ref, v_ref):
    m = _rms(mem_ref[...], g_ref[...]).astype(BF16)
    kv = _dot(m, w_ref[...])
    width = XA_HEADS * XA_HEAD
    for hd in range(XA_HEADS):
        sl = slice(hd * XA_HEAD, (hd + 1) * XA_HEAD)
        k_ref[:, sl] = _rms(kv[:, sl], gk_ref[...]).astype(BF16)
    v_ref[...] = kv[:, width:].astype(BF16)


def _memkv(mem, w):
    n = mem.shape[0]
    width = XA_HEADS * XA_HEAD
    blk = lambda cols: pl.BlockSpec((N_MEM, cols), lambda b: (b, 0))
    consts = [w["g_mem"], w["w_mem_kv"], w["g_xk"]]
    return pl.pallas_call(
        _memkv_body,
        grid=(n // N_MEM,),
        in_specs=[blk(D_MODEL)] + [_const_spec(a.shape) for a in consts],
        out_specs=[blk(width), blk(width)],
        out_shape=[jax.ShapeDtypeStruct((n, width), BF16)] * 2,
        compiler_params=_params(1),
        name="memkv",
    )(mem, *consts)


def _mix_body(x_ref, xprev_ref, xnext_ref, o_ref, mk_ref, mv_ref, gmix_ref, win_ref,
              convw_ref, gxq_ref, womla_ref, woconv_ref, womem_ref,
              wout_ref, y_ref, u_ref, *, tiles_per_seq):
    i = pl.program_id(0)
    tm = x_ref.shape[0]
    x = x_ref[...]
    h = _rms(x, gmix_ref[...]).astype(BF16)

    o_cc, o_cx, o_xq = CONV_WIDTH, 2 * CONV_WIDTH, 3 * CONV_WIDTH
    o_gate = o_xq + XA_HEADS * XA_HEAD
    proj = lambda hh, start, width: _dot(hh, win_ref[:, start:start + width])

    def gated(hh):
        return proj(hh, o_cc, CONV_WIDTH) * proj(hh, o_cx, CONV_WIDTH)

    first = (i % tiles_per_seq) == 0
    last = (i % tiles_per_seq) == tiles_per_seq - 1
    h_prev = _rms(xprev_ref[...], gmix_ref[...]).astype(BF16)
    h_next = _rms(xnext_ref[...], gmix_ref[...]).astype(BF16)
    u_ref[0:8, :] = jnp.where(first, 0.0, gated(h_prev))
    u_ref[8:8 + tm, :] = gated(h)
    u_ref[8 + tm:16 + tm, :] = jnp.where(last, 0.0, gated(h_next))
    cw = convw_ref[...]
    conv = (u_ref[7:7 + tm, :] * cw[0:1, :] + u_ref[8:8 + tm, :] * cw[1:2, :]
            + u_ref[9:9 + tm, :] * cw[2:3, :])
    y_conv = _dot((proj(h, 0, CONV_WIDTH) * conv).astype(BF16), woconv_ref[...])

    xq = proj(h, o_xq, XA_HEADS * XA_HEAD)
    heads = []
    for hd in range(XA_HEADS):
        sl = slice(hd * XA_HEAD, (hd + 1) * XA_HEAD)
        qh = (_rms(xq[:, sl], gxq_ref[...]) * (XA_HEAD ** -0.5)).astype(BF16)
        s = _dot_nt(qh, mk_ref[:, sl])
        p = jnp.exp(s - jnp.max(s, axis=-1, keepdims=True))
        l = jnp.sum(p, axis=-1, keepdims=True)
        heads.append((_dot(p.astype(BF16), mv_ref[:, sl]) / l).astype(BF16))
    y_mem = _dot(jnp.concatenate(heads, axis=-1), womem_ref[...])

    y_mla = _dot(o_ref[...], womla_ref[...])

    gate = lambda b: jax.nn.sigmoid(proj(h, o_gate + b * D_MODEL, D_MODEL))
    merged = gate(0) * y_mla + gate(1) * y_conv + gate(2) * y_mem
    y_ref[...] = x + _dot(merged.astype(BF16), wout_ref[...])


def _mix(x1, o, mk, mv, seq, w, tm):
    n = x1.shape[0]
    tiles_per_seq = seq // tm
    halo = tm // 8
    n_halo = n // 8
    tok = lambda width: pl.BlockSpec((tm, width), lambda i: (i, 0))
    mem = pl.BlockSpec((N_MEM, XA_HEADS * XA_HEAD), lambda i: (i // tiles_per_seq, 0))
    consts = [w["g_mix"], w["w_in_mix"], w["conv_w"],
              w["g_xq"], w["w_o_mla"], w["w_o_conv"], w["w_o_mem"], w["w_out"]]
    return pl.pallas_call(
        functools.partial(_mix_body, tiles_per_seq=tiles_per_seq),
        grid=(n // tm,),
        in_specs=[
            tok(D_MODEL),
            pl.BlockSpec((8, D_MODEL), lambda i: (jnp.maximum(i * halo - 1, 0), 0)),
            pl.BlockSpec((8, D_MODEL), lambda i: (jnp.minimum((i + 1) * halo, n_halo - 1), 0)),
            tok(MLA_HEADS * V_HEAD), mem, mem,
        ] + [_const_spec(a.shape) for a in consts],
        out_specs=tok(D_MODEL),
        out_shape=jax.ShapeDtypeStruct((n, D_MODEL), F32),
        scratch_shapes=[pltpu.VMEM((tm + 16, CONV_WIDTH), F32)],
        compiler_params=_params(1),
        name="mix",
    )(x1, x1, x1, o, mk, mv, *consts)


def _pad_heads(w, heads, width):
    k = w.shape[0]
    w = w.reshape(k, heads, width)
    return jnp.pad(w, ((0, 0), (0, 0), (0, HEAD_PAD - width))).reshape(k, heads * HEAD_PAD)


def _ffn_weights(norm, w_gu, w_down):
    return norm.reshape(1, D_MODEL), w_gu.astype(BF16), w_down.astype(BF16)


def _rope_tables(seq):
    half = QK_ROPE // 2
    inv_freq = ROPE_BASE ** (-jnp.arange(half, dtype=F32) / half)
    ang = jnp.arange(seq, dtype=jnp.int32).astype(F32)[:, None] * inv_freq[None, :]
    cos, sin = jnp.cos(ang), jnp.sin(ang)
    zeros = lambda width: jnp.zeros((seq, width), F32)
    tail = HEAD_PAD - QK_HEAD
    cos_t = jnp.concatenate([jnp.ones((seq, QK_NOPE), F32), cos, cos, zeros(tail)], axis=1)
    sin_t = jnp.concatenate([zeros(QK_NOPE), -sin, sin, zeros(tail)], axis=1)
    return cos_t, sin_t, cos.T, sin.T


def _trunk(x, mem, w, ffn1, ffn2, tables):
    batch, seq, _ = x.shape
    x = x.reshape(batch * seq, D_MODEL)
    x1 = _ffn(x, *ffn1, tm=512)
    qt, k, vt = _qkv(x1, seq, w, tables, tm=512)
    o = _attn(qt, k, vt, batch, seq)
    mk, mv = _memkv(mem.reshape(batch * N_MEM, D_MODEL), w)
    x2 = _mix(x1, o, mk, mv, seq, w, tm=512)
    y = _ffn(x2, *ffn2, tm=512)
    return y.reshape(batch, seq, D_MODEL)


def kernel(x_prompt, x_sample, mem_prompt, mem_sample, ffn1_norm, ffn1_w_gu, ffn1_w_down, mix_norm, w_in, q_lora_norm, w_uq, kv_lora_norm, w_uk, w_uv, mla_q_norm, mla_k_norm, w_o_mla, conv_w, w_o_conv, mem_norm, w_mem_kv, xa_q_norm, xa_k_norm, w_o_mem, w_out, ffn2_norm, ffn2_w_gu, ffn2_w_down):
    ffn1 = _ffn_weights(ffn1_norm[0], ffn1_w_gu[0], ffn1_w_down[0])
    ffn2 = _ffn_weights(ffn2_norm[0], ffn2_w_gu[0], ffn2_w_down[0])

    w_in0 = w_in[0]
    o_cq, o_ckv, o_kr = 0, Q_LORA, Q_LORA + KV_LORA
    o_cb = o_kr + QK_ROPE
    o_cc, o_cx = o_cb + CONV_WIDTH, o_cb + 2 * CONV_WIDTH
    o_xq = o_cb + 3 * CONV_WIDTH
    o_g = o_xq + XA_HEADS * XA_HEAD
    cols = lambda start, width: w_in0[:, start:start + width]
    half = QK_ROPE // 2
    swap_halves = lambda a: jnp.concatenate([a[..., half:], a[..., :half]], axis=-1)
    rope_lanes = lambda a: jnp.pad(a, ((0, 0), (QK_NOPE, HEAD_PAD - QK_HEAD)))
    kr_pad = rope_lanes(cols(o_kr, QK_ROPE))
    kr_swap_pad = rope_lanes(swap_halves(cols(o_kr, QK_ROPE)))
    gk_swap = rope_lanes(swap_halves(mla_k_norm[0][QK_NOPE:]).reshape(1, QK_ROPE))
    row = lambda g: g.reshape(1, -1)
    pad_gain = lambda g: jnp.pad(g, (0, HEAD_PAD - QK_HEAD))
    w = {
        "g_mix": row(mix_norm[0]),
        "w_a": jnp.concatenate([cols(o_cq, Q_LORA), cols(o_ckv, KV_LORA), kr_pad, kr_swap_pad],
                               axis=1).astype(BF16),
        "g_q": row(q_lora_norm[0]),
        "w_uq_t": _pad_heads(w_uq[0], MLA_HEADS, QK_HEAD).T.astype(BF16),
        "g_kv": row(kv_lora_norm[0]),
        "w_uk": _pad_heads(w_uk[0], MLA_HEADS, QK_NOPE).astype(BF16),
        "w_uv_t": w_uv[0].T.astype(BF16),
        "g_qh_col": pad_gain(mla_q_norm[0]).reshape(HEAD_PAD, 1),
        "g_kh": pad_gain(mla_k_norm[0]).reshape(1, HEAD_PAD),
        "g_kh_swap": gk_swap,
        "w_in_mix": w_in0[:, o_cb:].astype(BF16),
        "conv_w": conv_w[0],
        "g_xq": row(xa_q_norm[0]),
        "w_o_mla": w_o_mla[0].astype(BF16),
        "w_o_conv": w_o_conv[0].astype(BF16),
        "w_o_mem": w_o_mem[0].astype(BF16),
        "w_out": w_out[0].astype(BF16),
        "g_mem": row(mem_norm[0]),
        "w_mem_kv": w_mem_kv[0].astype(BF16),
        "g_xk": row(xa_k_norm[0]),
    }
    tables = _rope_tables(max(x_prompt.shape[1], x_sample.shape[1]))
    y_prompt = _trunk(x_prompt, mem_prompt, w, ffn1, ffn2, tables)
    y_sample = _trunk(x_sample, mem_sample, w, ffn1, ffn2, tables)
    return (y_prompt, y_sample)
```

```python
import functools

import jax
import jax.numpy as jnp
from jax import lax
from jax.experimental import pallas as pl
from jax.experimental.pallas import tpu as pltpu

D_MODEL = 1024
N_MEM = 256
MLA_HEADS = 8
QK_NOPE = 64
QK_ROPE = 32
QK_HEAD = QK_NOPE + QK_ROPE
V_HEAD = 64
Q_LORA = 384
KV_LORA = 256
CONV_WIDTH = 512
XA_HEADS = 4
XA_HEAD = 128
D_FF = 2816
ROPE_BASE = 10000.0
EPS = 1e-6
LOG2_E = 1.4426950408889634

LANES = 128
HEAD_PAD = LANES
FF_CHUNK = 256
N_FF_CHUNKS = D_FF // FF_CHUNK
KV_CHUNK = 512
Q_TILE = 512
HEAD_PASSES = ((0, 1),)
SUM_ROWS = 16
VMEM_LIMIT = 56 * 1024 * 1024

BF16 = jnp.bfloat16
F32 = jnp.float32


def _const_spec(shape):
    zeros = (0,) * len(shape)
    return pl.BlockSpec(shape, lambda *_: zeros, pipeline_mode=pl.Buffered(1))


def _params(n_axes):
    return pltpu.CompilerParams(
        dimension_semantics=("arbitrary",) * n_axes,
        vmem_limit_bytes=VMEM_LIMIT)


def _rms(x, gain, n=None):
    n = x.shape[-1] if n is None else n
    inv = lax.rsqrt(jnp.sum(x * x, axis=-1, keepdims=True) * (1.0 / n) + EPS)
    return (x * inv) * gain


def _dot(a, b):
    return jnp.dot(a, b, preferred_element_type=F32)


def _dot_nt(a, b):
    return lax.dot_general(a, b, (((1,), (1,)), ((), ())), preferred_element_type=F32)


def _ffn_body(x_ref, g_ref, wgu_ref, wd_ref, o_ref, xn_ref, acc_ref):
    xn_ref[...] = _rms(x_ref[...], g_ref[...]).astype(BF16)
    acc_ref[...] = jnp.zeros_like(acc_ref)
    for c in range(N_FF_CHUNKS):
        cols = slice(c * FF_CHUNK, (c + 1) * FF_CHUNK)
        up_cols = slice(D_FF + c * FF_CHUNK, D_FF + (c + 1) * FF_CHUNK)
        xn = xn_ref[...]
        g = _dot(xn, wgu_ref[:, cols])
        u = _dot(xn, wgu_ref[:, up_cols])
        a = (g * jax.nn.sigmoid(g) * u).astype(BF16)
        acc_ref[...] += _dot(a, wd_ref[cols, :])
    o_ref[...] = x_ref[...] + 0.5 * acc_ref[...]


def _ffn(x, gain, wgu, wd, tm):
    n = x.shape[0]
    return pl.pallas_call(
        _ffn_body,
        grid=(n // tm,),
        in_specs=[
            pl.BlockSpec((tm, D_MODEL), lambda i: (i, 0)),
            _const_spec((1, D_MODEL)),
            _const_spec(wgu.shape),
            _const_spec(wd.shape),
        ],
        out_specs=pl.BlockSpec((tm, D_MODEL), lambda i: (i, 0)),
        out_shape=jax.ShapeDtypeStruct((n, D_MODEL), F32),
        scratch_shapes=[pltpu.VMEM((tm, D_MODEL), BF16), pltpu.VMEM((tm, D_MODEL), F32)],
        compiler_params=_params(1),
        name="ffn",
    )(x, gain, wgu, wd)


def _qkv_body(x_ref, gmix_ref, wa_ref, gq_ref, wuqt_ref, gkv_ref, wuk_ref, wuvt_ref,
              gqh_ref, gkh_ref, gkhs_ref, cos_ref, sin_ref, cost_ref, sint_ref,
              qt_ref, k_ref, vt_ref):
    tm = x_ref.shape[0]
    half = QK_ROPE // 2
    h = _rms(x_ref[...], gmix_ref[...]).astype(BF16)
    c = _dot(h, wa_ref[...])
    cq = _rms(c[:, :Q_LORA], gq_ref[...]).astype(BF16)
    ckv = _rms(c[:, Q_LORA:Q_LORA + KV_LORA], gkv_ref[...]).astype(BF16)
    o_kr = Q_LORA + KV_LORA
    kr = c[:, o_kr:o_kr + HEAD_PAD]
    kr_swap = c[:, o_kr + HEAD_PAD:]

    vt = _dot_nt(wuvt_ref[...], ckv).astype(BF16)
    for ch in range(tm // KV_CHUNK):
        vt_ref[ch] = vt[:, ch * KV_CHUNK:(ch + 1) * KV_CHUNK]

    kn = _dot(ckv, wuk_ref[...])
    cos, sin = cos_ref[...], sin_ref[...]
    for hd in range(MLA_HEADS):
        sl = slice(hd * HEAD_PAD, (hd + 1) * HEAD_PAD)
        kh = kn[:, sl] + kr
        inv = lax.rsqrt(jnp.sum(kh * kh, axis=-1, keepdims=True) * (1.0 / QK_HEAD) + EPS)
        rot = ((kh * inv) * gkh_ref[...]) * cos + ((kr_swap * inv) * gkhs_ref[...]) * sin
        k_ref[:, sl] = rot.astype(BF16)

    qt = _dot_nt(wuqt_ref[...], cq)
    cost, sint = cost_ref[...], sint_ref[...]
    scale = QK_HEAD ** -0.5 * LOG2_E

    def put(row, rows_f32):
        for t in range(tm // Q_TILE):
            qt_ref[t, row:row + rows_f32.shape[0], :] = (
                rows_f32[:, t * Q_TILE:(t + 1) * Q_TILE].astype(BF16))

    for hd in range(MLA_HEADS):
        r0 = hd * HEAD_PAD
        blk = qt[r0:r0 + HEAD_PAD, :]
        inv = lax.rsqrt(jnp.sum(blk * blk, axis=0, keepdims=True) * (1.0 / QK_HEAD) + EPS)
        qn = (blk * inv) * gqh_ref[...]
        x1, x2 = qn[QK_NOPE:QK_NOPE + half], qn[QK_NOPE + half:QK_HEAD]
        put(r0, qn[:QK_NOPE] * scale)
        put(r0 + QK_NOPE, (x1 * cost - x2 * sint) * scale)
        put(r0 + QK_NOPE + half, (x2 * cost + x1 * sint) * scale)
        put(r0 + QK_HEAD, jnp.zeros((HEAD_PAD - QK_HEAD, tm), F32))


def _qkv(x1, seq, w, tables, tm):
    n = x1.shape[0]
    tiles_per_seq = seq // tm
    tok = lambda width: pl.BlockSpec((tm, width), lambda i: (i, 0))
    tab = pl.BlockSpec((tm, HEAD_PAD), lambda i: (i % tiles_per_seq, 0))
    tab_t = pl.BlockSpec((QK_ROPE // 2, tm), lambda i: (0, i % tiles_per_seq))
    consts = [w["g_mix"], w["w_a"], w["g_q"], w["w_uq_t"], w["g_kv"], w["w_uk"], w["w_uv_t"],
              w["g_qh_col"], w["g_kh"], w["g_kh_swap"]]
    return pl.pallas_call(
        _qkv_body,
        grid=(n // tm,),
        in_specs=([tok(D_MODEL)] + [_const_spec(a.shape) for a in consts]
                  + [tab, tab, tab_t, tab_t]),
        out_specs=[
            pl.BlockSpec((tm // Q_TILE, MLA_HEADS * HEAD_PAD, Q_TILE), lambda i: (i, 0, 0)),
            tok(MLA_HEADS * HEAD_PAD),
            pl.BlockSpec((tm // KV_CHUNK, MLA_HEADS * V_HEAD, KV_CHUNK), lambda i: (i, 0, 0)),
        ],
        out_shape=[jax.ShapeDtypeStruct((n // Q_TILE, MLA_HEADS * HEAD_PAD, Q_TILE), BF16),
                   jax.ShapeDtypeStruct((n, MLA_HEADS * HEAD_PAD), BF16),
                   jax.ShapeDtypeStruct((n // KV_CHUNK, MLA_HEADS * V_HEAD, KV_CHUNK), BF16)],
        compiler_params=_params(1),
        name="qkv",
    )(x1, *consts, *tables)


def _attn_body(qt_ref, k_ref, vt_ref, o_ref, s_ref, acc_ref, ot_ref):
    nq, _, tq = qt_ref.shape
    n_kv = vt_ref.shape[0]
    ones = jnp.ones((SUM_ROWS, KV_CHUNK), BF16)
    m0 = jnp.full((1, tq), -jnp.inf, F32)

    def run(hs):
        def scores(qi, j, slot):
            rows = pl.ds(pl.multiple_of(j * KV_CHUNK, KV_CHUNK), KV_CHUNK)
            cmax = []
            for hh in hs:
                st = _dot(k_ref[rows, hh * HEAD_PAD:(hh + 1) * HEAD_PAD],
                          qt_ref[qi, hh * HEAD_PAD:(hh + 1) * HEAD_PAD, :])
                s_ref[hh, slot] = st
                cmax.append(jnp.max(st, axis=0, keepdims=True))
            return tuple(cmax)

        def consume(j, slot, ms, cmax):
            out = []
            for n, hh in enumerate(hs):
                m_new = jnp.maximum(ms[n], cmax[n])
                alpha = jnp.exp2(ms[n] - m_new)
                pt = jnp.exp2(s_ref[hh, slot] - m_new).astype(BF16)
                vt = jnp.concatenate([vt_ref[j, hh * V_HEAD:(hh + 1) * V_HEAD, :], ones], axis=0)
                acc_ref[hh] = alpha * acc_ref[hh] + _dot(vt, pt)
                out.append(m_new)
            return tuple(out)

        def finish(qi):
            for hh in hs:
                ot_ref[qi, hh * V_HEAD:(hh + 1) * V_HEAD, :] = (
                    acc_ref[hh, :V_HEAD, :] / acc_ref[hh, V_HEAD:V_HEAD + 1, :])
                acc_ref[hh] = jnp.zeros(acc_ref.shape[1:], F32)

        def q_tile(qi, carry):
            def group(g, carry):
                ms, c0, c1 = carry
                j = 4 * g
                c2 = scores(qi, j + 2, 2)
                ms = consume(j, 0, ms, c0)
                c3 = scores(qi, j + 3, 3)
                ms = consume(j + 1, 1, ms, c1)
                c0 = scores(qi, j + 4, 0)
                ms = consume(j + 2, 2, ms, c2)
                c1 = scores(qi, j + 5, 1)
                ms = consume(j + 3, 3, ms, c3)
                return ms, c0, c1

            ms0 = (m0,) * len(hs)
            ms, c0, c1 = lax.fori_loop(0, n_kv // 4 - 1, group, (ms0,) + carry, unroll=True)
            q_next = jnp.minimum(qi + 1, nq - 1)
            c2 = scores(qi, n_kv - 2, 2)
            ms = consume(n_kv - 4, 0, ms, c0)
            c3 = scores(qi, n_kv - 1, 3)
            ms = consume(n_kv - 3, 1, ms, c1)
            c0 = scores(q_next, 0, 0)
            ms = consume(n_kv - 2, 2, ms, c2)
            c1 = scores(q_next, 1, 1)
            consume(n_kv - 1, 3, ms, c3)
            finish(qi)
            return c0, c1

        lax.fori_loop(0, nq, q_tile, (scores(0, 0, 0), scores(0, 1, 1)), unroll=2)

    acc_ref[...] = jnp.zeros_like(acc_ref)
    for hs in HEAD_PASSES:
        run(hs)

    def emit(qi, carry):
        rows = pl.ds(pl.multiple_of(qi * tq, tq), tq)
        o_ref[rows, :] = ot_ref[qi].T.astype(BF16)
        return carry

    lax.fori_loop(0, nq, emit, 0)


def _attn(qt, k, vt, batch, seq):
    n = k.shape[0]
    nq = seq // Q_TILE
    n_kv = seq // KV_CHUNK
    return pl.pallas_call(
        _attn_body,
        grid=(batch, MLA_HEADS // 2),
        in_specs=[
            pl.BlockSpec((nq, 2 * HEAD_PAD, Q_TILE), lambda b, hp: (b, hp, 0)),
            pl.BlockSpec((seq, 2 * HEAD_PAD), lambda b, hp: (b, hp)),
            pl.BlockSpec((n_kv, 2 * V_HEAD, KV_CHUNK), lambda b, hp: (b, hp, 0)),
        ],
        out_specs=pl.BlockSpec((seq, 2 * V_HEAD), lambda b, hp: (b, hp)),
        out_shape=jax.ShapeDtypeStruct((n, MLA_HEADS * V_HEAD), BF16),
        scratch_shapes=[pltpu.VMEM((2, 4, KV_CHUNK, Q_TILE), F32),
                        pltpu.VMEM((2, V_HEAD + SUM_ROWS, Q_TILE), F32),
                        pltpu.VMEM((nq, 2 * V_HEAD, Q_TILE), F32)],
        compiler_params=_params(2),
        name="attn",
    )(qt, k, vt)


def _memkv_body(mem_ref, g_ref, w_ref, gk_ref, k_ref, v_ref):
    m = _rms(mem_ref[...], g_ref[...]).astype(BF16)
    kv = _dot(m, w_ref[...])
    width = XA_HEADS * XA_HEAD
    for hd in range(XA_HEADS):
        sl = slice(hd * XA_HEAD, (hd + 1) * XA_HEAD)
        k_ref[:, sl] = _rms(kv[:, sl], gk_ref[...]).astype(BF16)
    v_ref[...] = kv[:, width:].astype(BF16)


def _memkv(mem, w):
    n = mem.shape[0]
    width = XA_HEADS * XA_HEAD
    blk = lambda cols: pl.BlockSpec((N_MEM, cols), lambda b: (b, 0))
    consts = [w["g_mem"], w["w_mem_kv"], w["g_xk"]]
    return pl.pallas_call(
        _memkv_body,
        grid=(n // N_MEM,),
        in_specs=[blk(D_MODEL)] + [_const_spec(a.shape) for a in consts],
        out_specs=[blk(width), blk(width)],
        out_shape=[jax.ShapeDtypeStruct((n, width), BF16)] * 2,
        compiler_params=_params(1),
        name="memkv",
    )(mem, *consts)


def _mix_body(x_ref, xprev_ref, xnext_ref, o_ref, mk_ref, mv_ref, gmix_ref, win_ref,
              convw_ref, gxq_ref, womla_ref, woconv_ref, womem_ref,
              wout_ref, y_ref, u_ref, *, tiles_per_seq):
    i = pl.program_id(0)
    tm = x_ref.shape[0]
    x = x_ref[...]
    h = _rms(x, gmix_ref[...]).astype(BF16)

    o_cc, o_cx, o_xq = CONV_WIDTH, 2 * CONV_WIDTH, 3 * CONV_WIDTH
    o_gate = o_xq + XA_HEADS * XA_HEAD
    proj = lambda hh, start, width: _dot(hh, win_ref[:, start:start + width])

    def gated(hh):
        return proj(hh, o_cc, CONV_WIDTH) * proj(hh, o_cx, CONV_WIDTH)

    first = (i % tiles_per_seq) == 0
    last = (i % tiles_per_seq) == tiles_per_seq - 1
    h_prev = _rms(xprev_ref[...], gmix_ref[...]).astype(BF16)
    h_next = _rms(xnext_ref[...], gmix_ref[...]).astype(BF16)
    u_ref[0:8, :] = jnp.where(first, 0.0, gated(h_prev))
    u_ref[8:8 + tm, :] = gated(h)
    u_ref[8 + tm:16 + tm, :] = jnp.where(last, 0.0, gated(h_next))
    cw = convw_ref[...]
    conv = (u_ref[7:7 + tm, :] * cw[0:1, :] + u_ref[8:8 + tm, :] * cw[1:2, :]
            + u_ref[9:9 + tm, :] * cw[2:3, :])
    y_conv = _dot((proj(h, 0, CONV_WIDTH) * conv).astype(BF16), woconv_ref[...])

    xq = proj(h, o_xq, XA_HEADS * XA_HEAD)
    heads = []
    for hd in range(XA_HEADS):
        sl = slice(hd * XA_HEAD, (hd + 1) * XA_HEAD)
        qh = (_rms(xq[:, sl], gxq_ref[...]) * (XA_HEAD ** -0.5)).astype(BF16)
        s = _dot_nt(qh, mk_ref[:, sl])
        p = jnp.exp(s - jnp.max(s, axis=-1, keepdims=True))
        l = jnp.sum(p, axis=-1, keepdims=True)
        heads.append((_dot(p.astype(BF16), mv_ref[:, sl]) / l).astype(BF16))
    y_mem = _dot(jnp.concatenate(heads, axis=-1), womem_ref[...])

    y_mla = _dot(o_ref[...], womla_ref[...])

    gate = lambda b: jax.nn.sigmoid(proj(h, o_gate + b * D_MODEL, D_MODEL))
    merged = gate(0) * y_mla + gate(1) * y_conv + gate(2) * y_mem
    y_ref[...] = x + _dot(merged.astype(BF16), wout_ref[...])


def _mix(x1, o, mk, mv, seq, w, tm):
    n = x1.shape[0]
    tiles_per_seq = seq // tm
    halo = tm // 8
    n_halo = n // 8
    tok = lambda width: pl.BlockSpec((tm, width), lambda i: (i, 0))
    mem = pl.BlockSpec((N_MEM, XA_HEADS * XA_HEAD), lambda i: (i // tiles_per_seq, 0))
    consts = [w["g_mix"], w["w_in_mix"], w["conv_w"],
              w["g_xq"], w["w_o_mla"], w["w_o_conv"], w["w_o_mem"], w["w_out"]]
    return pl.pallas_call(
        functools.partial(_mix_body, tiles_per_seq=tiles_per_seq),
        grid=(n // tm,),
        in_specs=[
            tok(D_MODEL),
            pl.BlockSpec((8, D_MODEL), lambda i: (jnp.maximum(i * halo - 1, 0), 0)),
            pl.BlockSpec((8, D_MODEL), lambda i: (jnp.minimum((i + 1) * halo, n_halo - 1), 0)),
            tok(MLA_HEADS * V_HEAD), mem, mem,
        ] + [_const_spec(a.shape) for a in consts],
        out_specs=tok(D_MODEL),
        out_shape=jax.ShapeDtypeStruct((n, D_MODEL), F32),
        scratch_shapes=[pltpu.VMEM((tm + 16, CONV_WIDTH), F32)],
        compiler_params=_params(1),
        name="mix",
    )(x1, x1, x1, o, mk, mv, *consts)


def _pad_heads(w, heads, width):
    k = w.shape[0]
    w = w.reshape(k, heads, width)
    return jnp.pad(w, ((0, 0), (0, 0), (0, HEAD_PAD - width))).reshape(k, heads * HEAD_PAD)


def _ffn_weights(norm, w_gu, w_down):
    return norm.reshape(1, D_MODEL), w_gu.astype(BF16), w_down.astype(BF16)


def _rope_tables(seq):
    half = QK_ROPE // 2
    inv_freq = ROPE_BASE ** (-jnp.arange(half, dtype=F32) / half)
    ang = jnp.arange(seq, dtype=jnp.int32).astype(F32)[:, None] * inv_freq[None, :]
    cos, sin = jnp.cos(ang), jnp.sin(ang)
    zeros = lambda width: jnp.zeros((seq, width), F32)
    tail = HEAD_PAD - QK_HEAD
    cos_t = jnp.concatenate([jnp.ones((seq, QK_NOPE), F32), cos, cos, zeros(tail)], axis=1)
    sin_t = jnp.concatenate([zeros(QK_NOPE), -sin, sin, zeros(tail)], axis=1)
    return cos_t, sin_t, cos.T, sin.T


def _trunk(x, mem, w, ffn1, ffn2, tables):
    batch, seq, _ = x.shape
    x = x.reshape(batch * seq, D_MODEL)
    x1 = _ffn(x, *ffn1, tm=512)
    qt, k, vt = _qkv(x1, seq, w, tables, tm=512)
    o = _attn(qt, k, vt, batch, seq)
    mk, mv = _memkv(mem.reshape(batch * N_MEM, D_MODEL), w)
    x2 = _mix(x1, o, mk, mv, seq, w, tm=512)
    y = _ffn(x2, *ffn2, tm=512)
    return y.reshape(batch, seq, D_MODEL)


def kernel(x_prompt, x_sample, mem_prompt, mem_sample, ffn1_norm, ffn1_w_gu, ffn1_w_down, mix_norm, w_in, q_lora_norm, w_uq, kv_lora_norm, w_uk, w_uv, mla_q_norm, mla_k_norm, w_o_mla, conv_w, w_o_conv, mem_norm, w_mem_kv, xa_q_norm, xa_k_norm, w_o_mem, w_out, ffn2_norm, ffn2_w_gu, ffn2_w_down):
    ffn1 = _ffn_weights(ffn1_norm[0], ffn1_w_gu[0], ffn1_w_down[0])
    ffn2 = _ffn_weights(ffn2_norm[0], ffn2_w_gu[0], ffn2_w_down[0])

    w_in0 = w_in[0]
    o_cq, o_ckv, o_kr = 0, Q_LORA, Q_LORA + KV_LORA
    o_cb = o_kr + QK_ROPE
    o_cc, o_cx = o_cb + CONV_WIDTH, o_cb + 2 * CONV_WIDTH
    o_xq = o_cb + 3 * CONV_WIDTH
    o_g = o_xq + XA_HEADS * XA_HEAD
    cols = lambda start, width: w_in0[:, start:start + width]
    half = QK_ROPE // 2
    swap_halves = lambda a: jnp.concatenate([a[..., half:], a[..., :half]], axis=-1)
    rope_lanes = lambda a: jnp.pad(a, ((0, 0), (QK_NOPE, HEAD_PAD - QK_HEAD)))
    kr_pad = rope_lanes(cols(o_kr, QK_ROPE))
    kr_swap_pad = rope_lanes(swap_halves(cols(o_kr, QK_ROPE)))
    gk_swap = rope_lanes(swap_halves(mla_k_norm[0][QK_NOPE:]).reshape(1, QK_ROPE))
    row = lambda g: g.reshape(1, -1)
    pad_gain = lambda g: jnp.pad(g, (0, HEAD_PAD - QK_HEAD))
    w = {
        "g_mix": row(mix_norm[0]),
        "w_a": jnp.concatenate([cols(o_cq, Q_LORA), cols(o_ckv, KV_LORA), kr_pad, kr_swap_pad],
                               axis=1).astype(BF16),
        "g_q": row(q_lora_norm[0]),
        "w_uq_t": _pad_heads(w_uq[0], MLA_HEADS, QK_HEAD).T.astype(BF16),
        "g_kv": row(kv_lora_norm[0]),
        "w_uk": _pad_heads(w_uk[0], MLA_HEADS, QK_NOPE).astype(BF16),
        "w_uv_t": w_uv[0].T.astype(BF16),
        "g_qh_col": pad_gain(mla_q_norm[0]).reshape(HEAD_PAD, 1),
        "g_kh": pad_gain(mla_k_norm[0]).reshape(1, HEAD_PAD),
        "g_kh_swap": gk_swap,
        "w_in_mix": w_in0[:, o_cb:].astype(BF16),
        "conv_w": conv_w[0],
        "g_xq": row(xa_q_norm[0]),
        "w_o_mla": w_o_mla[0].astype(BF16),
        "w_o_conv": w_o_conv[0].astype(BF16),
        "w_o_mem": w_o_mem[0].astype(BF16),
        "w_out": w_out[0].astype(BF16),
        "g_mem": row(mem_norm[0]),
        "w_mem_kv": w_mem_kv[0].astype(BF16),
        "g_xk": row(xa_k_norm[0]),
    }
    tables = _rope_tables(max(x_prompt.shape[1], x_sample.shape[1]))
    y_prompt = _trunk(x_prompt, mem_prompt, w, ffn1, ffn2, tables)
    y_sample = _trunk(x_sample, mem_sample, w, ffn1, ffn2, tables)
    return (y_prompt, y_sample)
```

```python
import functools

import jax
import jax.numpy as jnp
from jax import lax
from jax.experimental import pallas as pl
from jax.experimental.pallas import tpu as pltpu

D_MODEL = 1024
N_MEM = 256
MLA_HEADS = 8
QK_NOPE = 64
QK_ROPE = 32
QK_HEAD = QK_NOPE + QK_ROPE
V_HEAD = 64
Q_LORA = 384
KV_LORA = 256
CONV_WIDTH = 512
XA_HEADS = 4
XA_HEAD = 128
D_FF = 2816
ROPE_BASE = 10000.0
EPS = 1e-6
LOG2_E = 1.4426950408889634

LANES = 128
HEAD_PAD = LANES
FF_CHUNK = 256
N_FF_CHUNKS = D_FF // FF_CHUNK
KV_CHUNK = 512
Q_TILE = 512
SUM_ROWS = 16
VMEM_LIMIT = 56 * 1024 * 1024

BF16 = jnp.bfloat16
F32 = jnp.float32


def _const_spec(shape):
    zeros = (0,) * len(shape)
    return pl.BlockSpec(shape, lambda *_: zeros, pipeline_mode=pl.Buffered(1))


def _params(n_axes):
    return pltpu.CompilerParams(
        dimension_semantics=("arbitrary",) * n_axes,
        vmem_limit_bytes=VMEM_LIMIT)


def _rms(x, gain, n=None):
    n = x.shape[-1] if n is None else n
    inv = lax.rsqrt(jnp.sum(x * x, axis=-1, keepdims=True) * (1.0 / n) + EPS)
    return (x * inv) * gain


def _dot(a, b):
    return jnp.dot(a, b, preferred_element_type=F32)


def _dot_nt(a, b):
    return lax.dot_general(a, b, (((1,), (1,)), ((), ())), preferred_element_type=F32)


def _ffn_body(x_ref, g_ref, wgu_ref, wd_ref, o_ref, xn_ref, acc_ref):
    xn_ref[...] = _rms(x_ref[...], g_ref[...]).astype(BF16)
    acc_ref[...] = jnp.zeros_like(acc_ref)
    for c in range(N_FF_CHUNKS):
        cols = slice(c * FF_CHUNK, (c + 1) * FF_CHUNK)
        up_cols = slice(D_FF + c * FF_CHUNK, D_FF + (c + 1) * FF_CHUNK)
        xn = xn_ref[...]
        g = _dot(xn, wgu_ref[:, cols])
        u = _dot(xn, wgu_ref[:, up_cols])
        a = (g * jax.nn.sigmoid(g) * u).astype(BF16)
        acc_ref[...] += _dot(a, wd_ref[cols, :])
    o_ref[...] = x_ref[...] + 0.5 * acc_ref[...]


def _ffn(x, gain, wgu, wd, tm):
    n = x.shape[0]
    return pl.pallas_call(
        _ffn_body,
        grid=(n // tm,),
        in_specs=[
            pl.BlockSpec((tm, D_MODEL), lambda i: (i, 0)),
            _const_spec((1, D_MODEL)),
            _const_spec(wgu.shape),
            _const_spec(wd.shape),
        ],
        out_specs=pl.BlockSpec((tm, D_MODEL), lambda i: (i, 0)),
        out_shape=jax.ShapeDtypeStruct((n, D_MODEL), F32),
        scratch_shapes=[pltpu.VMEM((tm, D_MODEL), BF16), pltpu.VMEM((tm, D_MODEL), F32)],
        compiler_params=_params(1),
        name="ffn",
    )(x, gain, wgu, wd)


def _qkv_body(x_ref, gmix_ref, wa_ref, gq_ref, wuqt_ref, gkv_ref, wuk_ref, wuvt_ref,
              gqh_ref, gkh_ref, gkhs_ref, cos_ref, sin_ref, cost_ref, sint_ref,
              qt_ref, k_ref, vt_ref):
    tm = x_ref.shape[0]
    half = QK_ROPE // 2
    h = _rms(x_ref[...], gmix_ref[...]).astype(BF16)
    c = _dot(h, wa_ref[...])
    cq = _rms(c[:, :Q_LORA], gq_ref[...]).astype(BF16)
    ckv = _rms(c[:, Q_LORA:Q_LORA + KV_LORA], gkv_ref[...]).astype(BF16)
    o_kr = Q_LORA + KV_LORA
    kr = c[:, o_kr:o_kr + HEAD_PAD]
    kr_swap = c[:, o_kr + HEAD_PAD:]

    vt = _dot_nt(wuvt_ref[...], ckv).astype(BF16)
    for ch in range(tm // KV_CHUNK):
        vt_ref[ch] = vt[:, ch * KV_CHUNK:(ch + 1) * KV_CHUNK]

    kn = _dot(ckv, wuk_ref[...])
    cos, sin = cos_ref[...], sin_ref[...]
    for hd in range(MLA_HEADS):
        sl = slice(hd * HEAD_PAD, (hd + 1) * HEAD_PAD)
        kh = kn[:, sl] + kr
        inv = lax.rsqrt(jnp.sum(kh * kh, axis=-1, keepdims=True) * (1.0 / QK_HEAD) + EPS)
        rot = ((kh * inv) * gkh_ref[...]) * cos + ((kr_swap * inv) * gkhs_ref[...]) * sin
        k_ref[:, sl] = rot.astype(BF16)

    qt = _dot_nt(wuqt_ref[...], cq)
    cost, sint = cost_ref[...], sint_ref[...]
    scale = QK_HEAD ** -0.5 * LOG2_E

    def put(row, rows_f32):
        for t in range(tm // Q_TILE):
            qt_ref[t, row:row + rows_f32.shape[0], :] = (
                rows_f32[:, t * Q_TILE:(t + 1) * Q_TILE].astype(BF16))

    for hd in range(MLA_HEADS):
        r0 = hd * HEAD_PAD
        blk = qt[r0:r0 + HEAD_PAD, :]
        inv = lax.rsqrt(jnp.sum(blk * blk, axis=0, keepdims=True) * (1.0 / QK_HEAD) + EPS)
        qn = (blk * inv) * gqh_ref[...]
        x1, x2 = qn[QK_NOPE:QK_NOPE + half], qn[QK_NOPE + half:QK_HEAD]
        put(r0, qn[:QK_NOPE] * scale)
        put(r0 + QK_NOPE, (x1 * cost - x2 * sint) * scale)
        put(r0 + QK_NOPE + half, (x2 * cost + x1 * sint) * scale)
        put(r0 + QK_HEAD, jnp.zeros((HEAD_PAD - QK_HEAD, tm), F32))


def _qkv(x1, seq, w, tables, tm):
    n = x1.shape[0]
    tiles_per_seq = seq // tm
    tok = lambda width: pl.BlockSpec((tm, width), lambda i: (i, 0))
    tab = pl.BlockSpec((tm, HEAD_PAD), lambda i: (i % tiles_per_seq, 0))
    tab_t = pl.BlockSpec((QK_ROPE // 2, tm), lambda i: (0, i % tiles_per_seq))
    consts = [w["g_mix"], w["w_a"], w["g_q"], w["w_uq_t"], w["g_kv"], w["w_uk"], w["w_uv_t"],
              w["g_qh_col"], w["g_kh"], w["g_kh_swap"]]
    return pl.pallas_call(
        _qkv_body,
        grid=(n // tm,),
        in_specs=([tok(D_MODEL)] + [_const_spec(a.shape) for a in consts]
                  + [tab, tab, tab_t, tab_t]),
        out_specs=[
            pl.BlockSpec((tm // Q_TILE, MLA_HEADS * HEAD_PAD, Q_TILE), lambda i: (i, 0, 0)),
            tok(MLA_HEADS * HEAD_PAD),
            pl.BlockSpec((tm // KV_CHUNK, MLA_HEADS * V_HEAD, KV_CHUNK), lambda i: (i, 0, 0)),
        ],
        out_shape=[jax.ShapeDtypeStruct((n // Q_TILE, MLA_HEADS * HEAD_PAD, Q_TILE), BF16),
                   jax.ShapeDtypeStruct((n, MLA_HEADS * HEAD_PAD), BF16),
                   jax.ShapeDtypeStruct((n // KV_CHUNK, MLA_HEADS * V_HEAD, KV_CHUNK), BF16)],
        compiler_params=_params(1),
        name="qkv",
    )(x1, *consts, *tables)


def _attn_body(qt_ref, k_ref, vt_ref, o_ref, s_ref, p_ref, acc_ref, ot_ref):
    nq, _, tq = qt_ref.shape
    n_kv = vt_ref.shape[0]
    ones = jnp.ones((SUM_ROWS, KV_CHUNK), BF16)
    m0 = jnp.full((1, tq), -jnp.inf, F32)
    heads = range(2)

    def scores(qi, j):
        rows = pl.ds(pl.multiple_of(j * KV_CHUNK, KV_CHUNK), KV_CHUNK)
        cmax = []
        for hh in heads:
            st = _dot(k_ref[rows, hh * HEAD_PAD:(hh + 1) * HEAD_PAD],
                      qt_ref[qi, hh * HEAD_PAD:(hh + 1) * HEAD_PAD, :])
            s_ref[hh, j % 4] = st
            cmax.append(jnp.max(st, axis=0, keepdims=True))
        return tuple(cmax)

    def probs(j, ms, cmax):
        m_out, alpha = [], []
        for hh in heads:
            m_new = jnp.maximum(ms[hh], cmax[hh])
            alpha.append(jnp.exp2(ms[hh] - m_new))
            p_ref[hh, j % 4] = jnp.exp2(s_ref[hh, j % 4] - m_new).astype(BF16)
            m_out.append(m_new)
        return tuple(m_out), tuple(alpha)

    def values(j, alpha):
        for hh in heads:
            vt = jnp.concatenate([vt_ref[j, hh * V_HEAD:(hh + 1) * V_HEAD, :], ones], axis=0)
            acc_ref[hh] = alpha[hh] * acc_ref[hh] + _dot(vt, p_ref[hh, j % 4])

    def finish(qi):
        for hh in heads:
            ot_ref[qi, hh * V_HEAD:(hh + 1) * V_HEAD, :] = (
                acc_ref[hh, :V_HEAD, :] / acc_ref[hh, V_HEAD:V_HEAD + 1, :])
        acc_ref[...] = jnp.zeros_like(acc_ref)

    def q_tile(qi, carry):
        c_cur, c_nxt, alpha = carry
        q_prev = jnp.maximum(qi - 1, 0)
        q_next = jnp.minimum(qi + 1, nq - 1)
        ms = (m0, m0)
        for j in range(n_kv):
            ahead = j + 2
            c_new = scores(qi, ahead) if ahead < n_kv else scores(q_next, ahead - n_kv)
            ms, alpha_new = probs(j, ms, c_cur)
            values((j - 1) % n_kv, alpha)
            if j == 0:
                finish(q_prev)
            c_cur, c_nxt, alpha = c_nxt, c_new, alpha_new
        return c_cur, c_nxt, alpha

    acc_ref[...] = jnp.zeros_like(acc_ref)
    acc_ref[:, V_HEAD:, :] = jnp.ones((2, SUM_ROWS, tq), F32)
    p_ref[...] = jnp.zeros_like(p_ref)
    one = jnp.ones((1, tq), F32)
    carry = lax.fori_loop(0, nq, q_tile, (scores(0, 0), scores(0, 1), (one, one)), unroll=2)
    values(n_kv - 1, carry[2])
    finish(nq - 1)

    def emit(qi, carry):
        rows = pl.ds(pl.multiple_of(qi * tq, tq), tq)
        o_ref[rows, :] = ot_ref[qi].T.astype(BF16)
        return carry

    lax.fori_loop(0, nq, emit, 0)


def _attn(qt, k, vt, batch, seq):
    n = k.shape[0]
    nq = seq // Q_TILE
    n_kv = seq // KV_CHUNK
    return pl.pallas_call(
        _attn_body,
        grid=(batch, MLA_HEADS // 2),
        in_specs=[
            pl.BlockSpec((nq, 2 * HEAD_PAD, Q_TILE), lambda b, hp: (b, hp, 0)),
            pl.BlockSpec((seq, 2 * HEAD_PAD), lambda b, hp: (b, hp)),
            pl.BlockSpec((n_kv, 2 * V_HEAD, KV_CHUNK), lambda b, hp: (b, hp, 0)),
        ],
        out_specs=pl.BlockSpec((seq, 2 * V_HEAD), lambda b, hp: (b, hp)),
        out_shape=jax.ShapeDtypeStruct((n, MLA_HEADS * V_HEAD), BF16),
        scratch_shapes=[pltpu.VMEM((2, 4, KV_CHUNK, Q_TILE), F32),
                        pltpu.VMEM((2, 4, KV_CHUNK, Q_TILE), BF16),
                        pltpu.VMEM((2, V_HEAD + SUM_ROWS, Q_TILE), F32),
                        pltpu.VMEM((nq, 2 * V_HEAD, Q_TILE), F32)],
        compiler_params=_params(2),
        name="attn",
    )(qt, k, vt)


def _memkv_body(mem_ref, g_ref, w_ref, gk_ref, k_ref, v_ref):
    m = _rms(mem_ref[...], g_ref[...]).astype(BF16)
    kv = _dot(m, w_ref[...])
    width = XA_HEADS * XA_HEAD
    for hd in range(XA_HEADS):
        sl = slice(hd * XA_HEAD, (hd + 1) * XA_HEAD)
        k_ref[:, sl] = _rms(kv[:, sl], gk_ref[...]).astype(BF16)
    v_ref[...] = kv[:, width:].astype(BF16)


def _memkv(mem, w):
    n = mem.shape[0]
    width = XA_HEADS * XA_HEAD
    blk = lambda cols: pl.BlockSpec((N_MEM, cols), lambda b: (b, 0))
    consts = [w["g_mem"], w["w_mem_kv"], w["g_xk"]]
    return pl.pallas_call(
        _memkv_body,
        grid=(n // N_MEM,),
        in_specs=[blk(D_MODEL)] + [_const_spec(a.shape) for a in consts],
        out_specs=[blk(width), blk(width)],
        out_shape=[jax.ShapeDtypeStruct((n, width), BF16)] * 2,
        compiler_params=_params(1),
        name="memkv",
    )(mem, *consts)


def _mix_body(x_ref, xprev_ref, xnext_ref, o_ref, mk_ref, mv_ref, gmix_ref, win_ref,
              convw_ref, gxq_ref, womla_ref, woconv_ref, womem_ref,
              wout_ref, y_ref, u_ref, *, tiles_per_seq):
    i = pl.program_id(0)
    tm = x_ref.shape[0]
    x = x_ref[...]
    h = _rms(x, gmix_ref[...]).astype(BF16)

    o_cc, o_cx, o_xq = CONV_WIDTH, 2 * CONV_WIDTH, 3 * CONV_WIDTH
    o_gate = o_xq + XA_HEADS * XA_HEAD
    proj = lambda hh, start, width: _dot(hh, win_ref[:, start:start + width])

    def gated(hh):
        return proj(hh, o_cc, CONV_WIDTH) * proj(hh, o_cx, CONV_WIDTH)

    first = (i % tiles_per_seq) == 0
    last = (i % tiles_per_seq) == tiles_per_seq - 1
    h_prev = _rms(xprev_ref[...], gmix_ref[...]).astype(BF16)
    h_next = _rms(xnext_ref[...], gmix_ref[...]).astype(BF16)
    u_ref[0:8, :] = jnp.where(first, 0.0, gated(h_prev))
    u_ref[8:8 + tm, :] = gated(h)
    u_ref[8 + tm:16 + tm, :] = jnp.where(last, 0.0, gated(h_next))
    cw = convw_ref[...]
    conv = (u_ref[7:7 + tm, :] * cw[0:1, :] + u_ref[8:8 + tm, :] * cw[1:2, :]
            + u_ref[9:9 + tm, :] * cw[2:3, :])
    y_conv = _dot((proj(h, 0, CONV_WIDTH) * conv).astype(BF16), woconv_ref[...])

    xq = proj(h, o_xq, XA_HEADS * XA_HEAD)
    heads = []
    for hd in range(XA_HEADS):
        sl = slice(hd * XA_HEAD, (hd + 1) * XA_HEAD)
        qh = (_rms(xq[:, sl], gxq_ref[...]) * (XA_HEAD ** -0.5)).astype(BF16)
        s = _dot_nt(qh, mk_ref[:, sl])
        p = jnp.exp(s - jnp.max(s, axis=-1, keepdims=True))
        l = jnp.sum(p, axis=-1, keepdims=True)
        heads.append((_dot(p.astype(BF16), mv_ref[:, sl]) / l).astype(BF16))
    y_mem = _dot(jnp.concatenate(heads, axis=-1), womem_ref[...])

    y_mla = _dot(o_ref[...], womla_ref[...])

    gate = lambda b: jax.nn.sigmoid(proj(h, o_gate + b * D_MODEL, D_MODEL))
    merged = gate(0) * y_mla + gate(1) * y_conv + gate(2) * y_mem
    y_ref[...] = x + _dot(merged.astype(BF16), wout_ref[...])


def _mix(x1, o, mk, mv, seq, w, tm):
    n = x1.shape[0]
    tiles_per_seq = seq // tm
    halo = tm // 8
    n_halo = n // 8
    tok = lambda width: pl.BlockSpec((tm, width), lambda i: (i, 0))
    mem = pl.BlockSpec((N_MEM, XA_HEADS * XA_HEAD), lambda i: (i // tiles_per_seq, 0))
    consts = [w["g_mix"], w["w_in_mix"], w["conv_w"],
              w["g_xq"], w["w_o_mla"], w["w_o_conv"], w["w_o_mem"], w["w_out"]]
    return pl.pallas_call(
        functools.partial(_mix_body, tiles_per_seq=tiles_per_seq),
        grid=(n // tm,),
        in_specs=[
            tok(D_MODEL),
            pl.BlockSpec((8, D_MODEL), lambda i: (jnp.maximum(i * halo - 1, 0), 0)),
            pl.BlockSpec((8, D_MODEL), lambda i: (jnp.minimum((i + 1) * halo, n_halo - 1), 0)),
            tok(MLA_HEADS * V_HEAD), mem, mem,
        ] + [_const_spec(a.shape) for a in consts],
        out_specs=tok(D_MODEL),
        out_shape=jax.ShapeDtypeStruct((n, D_MODEL), F32),
        scratch_shapes=[pltpu.VMEM((tm + 16, CONV_WIDTH), F32)],
        compiler_params=_params(1),
        name="mix",
    )(x1, x1, x1, o, mk, mv, *consts)


def _pad_heads(w, heads, width):
    k = w.shape[0]
    w = w.reshape(k, heads, width)
    return jnp.pad(w, ((0, 0), (0, 0), (0, HEAD_PAD - width))).reshape(k, heads * HEAD_PAD)


def _ffn_weights(norm, w_gu, w_down):
    return norm.reshape(1, D_MODEL), w_gu.astype(BF16), w_down.astype(BF16)


def _rope_tables(seq):
    half = QK_ROPE // 2
    inv_freq = ROPE_BASE ** (-jnp.arange(half, dtype=F32) / half)
    ang = jnp.arange(seq, dtype=jnp.int32).astype(F32)[:, None] * inv_freq[None, :]
    cos, sin = jnp.cos(ang), jnp.sin(ang)
    zeros = lambda width: jnp.zeros((seq, width), F32)
    tail = HEAD_PAD - QK_HEAD
    cos_t = jnp.concatenate([jnp.ones((seq, QK_NOPE), F32), cos, cos, zeros(tail)], axis=1)
    sin_t = jnp.concatenate([zeros(QK_NOPE), -sin, sin, zeros(tail)], axis=1)
    return cos_t, sin_t, cos.T, sin.T


def _trunk(x, mem, w, ffn1, ffn2, tables):
    batch, seq, _ = x.shape
    x = x.reshape(batch * seq, D_MODEL)
    x1 = _ffn(x, *ffn1, tm=512)
    qt, k, vt = _qkv(x1, seq, w, tables, tm=512)
    o = _attn(qt, k, vt, batch, seq)
    mk, mv = _memkv(mem.reshape(batch * N_MEM, D_MODEL), w)
    x2 = _mix(x1, o, mk, mv, seq, w, tm=512)
    y = _ffn(x2, *ffn2, tm=512)
    return y.reshape(batch, seq, D_MODEL)


def kernel(x_prompt, x_sample, mem_prompt, mem_sample, ffn1_norm, ffn1_w_gu, ffn1_w_down, mix_norm, w_in, q_lora_norm, w_uq, kv_lora_norm, w_uk, w_uv, mla_q_norm, mla_k_norm, w_o_mla, conv_w, w_o_conv, mem_norm, w_mem_kv, xa_q_norm, xa_k_norm, w_o_mem, w_out, ffn2_norm, ffn2_w_gu, ffn2_w_down):
    ffn1 = _ffn_weights(ffn1_norm[0], ffn1_w_gu[0], ffn1_w_down[0])
    ffn2 = _ffn_weights(ffn2_norm[0], ffn2_w_gu[0], ffn2_w_down[0])

    w_in0 = w_in[0]
    o_cq, o_ckv, o_kr = 0, Q_LORA, Q_LORA + KV_LORA
    o_cb = o_kr + QK_ROPE
    o_cc, o_cx = o_cb + CONV_WIDTH, o_cb + 2 * CONV_WIDTH
    o_xq = o_cb + 3 * CONV_WIDTH
    o_g = o_xq + XA_HEADS * XA_HEAD
    cols = lambda start, width: w_in0[:, start:start + width]
    half = QK_ROPE // 2
    swap_halves = lambda a: jnp.concatenate([a[..., half:], a[..., :half]], axis=-1)
    rope_lanes = lambda a: jnp.pad(a, ((0, 0), (QK_NOPE, HEAD_PAD - QK_HEAD)))
    kr_pad = rope_lanes(cols(o_kr, QK_ROPE))
    kr_swap_pad = rope_lanes(swap_halves(cols(o_kr, QK_ROPE)))
    gk_swap = rope_lanes(swap_halves(mla_k_norm[0][QK_NOPE:]).reshape(1, QK_ROPE))
    row = lambda g: g.reshape(1, -1)
    pad_gain = lambda g: jnp.pad(g, (0, HEAD_PAD - QK_HEAD))
    w = {
        "g_mix": row(mix_norm[0]),
        "w_a": jnp.concatenate([cols(o_cq, Q_LORA), cols(o_ckv, KV_LORA), kr_pad, kr_swap_pad],
                               axis=1).astype(BF16),
        "g_q": row(q_lora_norm[0]),
        "w_uq_t": _pad_heads(w_uq[0], MLA_HEADS, QK_HEAD).T.astype(BF16),
        "g_kv": row(kv_lora_norm[0]),
        "w_uk": _pad_heads(w_uk[0], MLA_HEADS, QK_NOPE).astype(BF16),
        "w_uv_t": w_uv[0].T.astype(BF16),
        "g_qh_col": pad_gain(mla_q_norm[0]).reshape(HEAD_PAD, 1),
        "g_kh": pad_gain(mla_k_norm[0]).reshape(1, HEAD_PAD),
        "g_kh_swap": gk_swap,
        "w_in_mix": w_in0[:, o_cb:].astype(BF16),
        "conv_w": conv_w[0],
        "g_xq": row(xa_q_norm[0]),
        "w_o_mla": w_o_mla[0].astype(BF16),
        "w_o_conv": w_o_conv[0].astype(BF16),
        "w_o_mem": w_o_mem[0].astype(BF16),
        "w_out": w_out[0].astype(BF16),
        "g_mem": row(mem_norm[0]),
        "w_mem_kv": w_mem_kv[0].astype(BF16),
        "g_xk": row(xa_k_norm[0]),
    }
    tables = _rope_tables(max(x_prompt.shape[1], x_sample.shape[1]))
    y_prompt = _trunk(x_prompt, mem_prompt, w, ffn1, ffn2, tables)
    y_sample = _trunk(x_sample, mem_sample, w, ffn1, ffn2, tables)
    return (y_prompt, y_sample)
```

```python
import functools

import jax
import jax.numpy as jnp
from jax import lax
from jax.experimental import pallas as pl
from jax.experimental.pallas import tpu as pltpu

D_MODEL = 1024
N_MEM = 256
MLA_HEADS = 8
QK_NOPE = 64
QK_ROPE = 32
QK_HEAD = QK_NOPE + QK_ROPE
V_HEAD = 64
Q_LORA = 384
KV_LORA = 256
CONV_WIDTH = 512
XA_HEADS = 4
XA_HEAD = 128
D_FF = 2816
ROPE_BASE = 10000.0
EPS = 1e-6
LOG2_E = 1.4426950408889634

LANES = 128
HEAD_PAD = LANES
FF_CHUNK = 256
N_FF_CHUNKS = D_FF // FF_CHUNK
KV_CHUNK = 512
Q_TILE = 512
SUM_ROWS = 16
PLAIN_KEYS = 256
MAX_PLAIN_LOGIT = 60.0
VMEM_LIMIT = 56 * 1024 * 1024

BF16 = jnp.bfloat16
F32 = jnp.float32


def _const_spec(shape):
    zeros = (0,) * len(shape)
    return pl.BlockSpec(shape, lambda *_: zeros, pipeline_mode=pl.Buffered(1))


def _params(n_axes):
    return pltpu.CompilerParams(
        dimension_semantics=("arbitrary",) * n_axes,
        vmem_limit_bytes=VMEM_LIMIT)


def _rms(x, gain, n=None):
    n = x.shape[-1] if n is None else n
    inv = lax.rsqrt(jnp.sum(x * x, axis=-1, keepdims=True) * (1.0 / n) + EPS)
    return (x * inv) * gain


def _dot(a, b):
    return jnp.dot(a, b, preferred_element_type=F32)


def _dot_nt(a, b):
    return lax.dot_general(a, b, (((1,), (1,)), ((), ())), preferred_element_type=F32)


def _ffn_body(x_ref, g_ref, wgu_ref, wd_ref, o_ref, xn_ref, acc_ref):
    xn_ref[...] = _rms(x_ref[...], g_ref[...]).astype(BF16)
    acc_ref[...] = jnp.zeros_like(acc_ref)
    for c in range(N_FF_CHUNKS):
        cols = slice(c * FF_CHUNK, (c + 1) * FF_CHUNK)
        up_cols = slice(D_FF + c * FF_CHUNK, D_FF + (c + 1) * FF_CHUNK)
        xn = xn_ref[...]
        g = _dot(xn, wgu_ref[:, cols])
        u = _dot(xn, wgu_ref[:, up_cols])
        a = (g * jax.nn.sigmoid(g) * u).astype(BF16)
        acc_ref[...] += _dot(a, wd_ref[cols, :])
    o_ref[...] = x_ref[...] + 0.5 * acc_ref[...]


def _ffn(x, gain, wgu, wd, tm):
    n = x.shape[0]
    return pl.pallas_call(
        _ffn_body,
        grid=(n // tm,),
        in_specs=[
            pl.BlockSpec((tm, D_MODEL), lambda i: (i, 0)),
            _const_spec((1, D_MODEL)),
            _const_spec(wgu.shape),
            _const_spec(wd.shape),
        ],
        out_specs=pl.BlockSpec((tm, D_MODEL), lambda i: (i, 0)),
        out_shape=jax.ShapeDtypeStruct((n, D_MODEL), F32),
        scratch_shapes=[pltpu.VMEM((tm, D_MODEL), BF16), pltpu.VMEM((tm, D_MODEL), F32)],
        compiler_params=_params(1),
        name="ffn",
    )(x, gain, wgu, wd)


def _qkv_body(x_ref, gmix_ref, wa_ref, gq_ref, wuqt_ref, gkv_ref, wuk_ref, wuvt_ref,
              gqh_ref, gkh_ref, gkhs_ref, cos_ref, sin_ref, cost_ref, sint_ref,
              qt_ref, k_ref, vt_ref):
    tm = x_ref.shape[0]
    half = QK_ROPE // 2
    h = _rms(x_ref[...], gmix_ref[...]).astype(BF16)
    c = _dot(h, wa_ref[...])
    cq = _rms(c[:, :Q_LORA], gq_ref[...]).astype(BF16)
    ckv = _rms(c[:, Q_LORA:Q_LORA + KV_LORA], gkv_ref[...]).astype(BF16)
    o_kr = Q_LORA + KV_LORA
    kr = c[:, o_kr:o_kr + HEAD_PAD]
    kr_swap = c[:, o_kr + HEAD_PAD:]

    vt = _dot_nt(wuvt_ref[...], ckv).astype(BF16)
    for ch in range(tm // KV_CHUNK):
        vt_ref[ch] = vt[:, ch * KV_CHUNK:(ch + 1) * KV_CHUNK]

    kn = _dot(ckv, wuk_ref[...])
    cos, sin = cos_ref[...], sin_ref[...]
    for hd in range(MLA_HEADS):
        sl = slice(hd * HEAD_PAD, (hd + 1) * HEAD_PAD)
        kh = kn[:, sl] + kr
        inv = lax.rsqrt(jnp.sum(kh * kh, axis=-1, keepdims=True) * (1.0 / QK_HEAD) + EPS)
        rot = ((kh * inv) * gkh_ref[...]) * cos + ((kr_swap * inv) * gkhs_ref[...]) * sin
        k_ref[:, sl] = rot.astype(BF16)

    qt = _dot_nt(wuqt_ref[...], cq)
    cost, sint = cost_ref[...], sint_ref[...]
    scale = QK_HEAD ** -0.5 * LOG2_E

    def put(row, rows_f32):
        for t in range(tm // Q_TILE):
            qt_ref[t, row:row + rows_f32.shape[0], :] = (
                rows_f32[:, t * Q_TILE:(t + 1) * Q_TILE].astype(BF16))

    for hd in range(MLA_HEADS):
        r0 = hd * HEAD_PAD
        blk = qt[r0:r0 + HEAD_PAD, :]
        inv = lax.rsqrt(jnp.sum(blk * blk, axis=0, keepdims=True) * (1.0 / QK_HEAD) + EPS)
        qn = (blk * inv) * gqh_ref[...]
        x1, x2 = qn[QK_NOPE:QK_NOPE + half], qn[QK_NOPE + half:QK_HEAD]
        put(r0, qn[:QK_NOPE] * scale)
        put(r0 + QK_NOPE, (x1 * cost - x2 * sint) * scale)
        put(r0 + QK_NOPE + half, (x2 * cost + x1 * sint) * scale)
        put(r0 + QK_HEAD, jnp.zeros((HEAD_PAD - QK_HEAD, tm), F32))


def _qkv(x1, seq, w, tables, tm):
    n = x1.shape[0]
    tiles_per_seq = seq // tm
    tok = lambda width: pl.BlockSpec((tm, width), lambda i: (i, 0))
    tab = pl.BlockSpec((tm, HEAD_PAD), lambda i: (i % tiles_per_seq, 0))
    tab_t = pl.BlockSpec((QK_ROPE // 2, tm), lambda i: (0, i % tiles_per_seq))
    consts = [w["g_mix"], w["w_a"], w["g_q"], w["w_uq_t"], w["g_kv"], w["w_uk"], w["w_uv_t"],
              w["g_qh_col"], w["g_kh"], w["g_kh_swap"]]
    return pl.pallas_call(
        _qkv_body,
        grid=(n // tm,),
        in_specs=([tok(D_MODEL)] + [_const_spec(a.shape) for a in consts]
                  + [tab, tab, tab_t, tab_t]),
        out_specs=[
            pl.BlockSpec((tm // Q_TILE, MLA_HEADS * HEAD_PAD, Q_TILE), lambda i: (i, 0, 0)),
            tok(MLA_HEADS * HEAD_PAD),
            pl.BlockSpec((tm // KV_CHUNK, MLA_HEADS * V_HEAD, KV_CHUNK), lambda i: (i, 0, 0)),
        ],
        out_shape=[jax.ShapeDtypeStruct((n // Q_TILE, MLA_HEADS * HEAD_PAD, Q_TILE), BF16),
                   jax.ShapeDtypeStruct((n, MLA_HEADS * HEAD_PAD), BF16),
                   jax.ShapeDtypeStruct((n // KV_CHUNK, MLA_HEADS * V_HEAD, KV_CHUNK), BF16)],
        compiler_params=_params(1),
        name="qkv",
    )(x1, *consts, *tables)


def _attn_flash_body(qt_ref, k_ref, vt_ref, o_ref, s_ref, acc_ref, ot_ref):
    nq, _, tq = qt_ref.shape
    n_kv = vt_ref.shape[0]
    ones = jnp.ones((SUM_ROWS, KV_CHUNK), BF16)
    m0 = jnp.full((1, tq), -jnp.inf, F32)
    heads = range(2)

    def scores(qi, j, slot):
        rows = pl.ds(pl.multiple_of(j * KV_CHUNK, KV_CHUNK), KV_CHUNK)
        cmax = []
        for hh in heads:
            st = _dot(k_ref[rows, hh * HEAD_PAD:(hh + 1) * HEAD_PAD],
                      qt_ref[qi, hh * HEAD_PAD:(hh + 1) * HEAD_PAD, :])
            s_ref[hh, slot] = st
            cmax.append(jnp.max(st, axis=0, keepdims=True))
        return tuple(cmax)

    def consume(j, slot, ms, cmax):
        out = []
        for hh in heads:
            m_new = jnp.maximum(ms[hh], cmax[hh])
            alpha = jnp.exp2(ms[hh] - m_new)
            pt = jnp.exp2(s_ref[hh, slot] - m_new).astype(BF16)
            vt = jnp.concatenate([vt_ref[j, hh * V_HEAD:(hh + 1) * V_HEAD, :], ones], axis=0)
            acc_ref[hh] = alpha * acc_ref[hh] + _dot(vt, pt)
            out.append(m_new)
        return tuple(out)

    def finish(qi):
        for hh in heads:
            ot_ref[qi, hh * V_HEAD:(hh + 1) * V_HEAD, :] = (
                acc_ref[hh, :V_HEAD, :] / acc_ref[hh, V_HEAD:V_HEAD + 1, :])
        acc_ref[...] = jnp.zeros_like(acc_ref)

    def q_tile(qi, carry):
        def group(g, carry):
            ms, c0, c1 = carry
            j = 4 * g
            c2 = scores(qi, j + 2, 2)
            ms = consume(j, 0, ms, c0)
            c3 = scores(qi, j + 3, 3)
            ms = consume(j + 1, 1, ms, c1)
            c0 = scores(qi, j + 4, 0)
            ms = consume(j + 2, 2, ms, c2)
            c1 = scores(qi, j + 5, 1)
            ms = consume(j + 3, 3, ms, c3)
            return ms, c0, c1

        ms, c0, c1 = lax.fori_loop(0, n_kv // 4 - 1, group, ((m0, m0),) + carry, unroll=True)
        q_next = jnp.minimum(qi + 1, nq - 1)
        c2 = scores(qi, n_kv - 2, 2)
        ms = consume(n_kv - 4, 0, ms, c0)
        c3 = scores(qi, n_kv - 1, 3)
        ms = consume(n_kv - 3, 1, ms, c1)
        c0 = scores(q_next, 0, 0)
        ms = consume(n_kv - 2, 2, ms, c2)
        c1 = scores(q_next, 1, 1)
        consume(n_kv - 1, 3, ms, c3)
        finish(qi)
        return c0, c1

    acc_ref[...] = jnp.zeros_like(acc_ref)
    lax.fori_loop(0, nq, q_tile, (scores(0, 0, 0), scores(0, 1, 1)), unroll=2)

    def emit(qi, carry):
        rows = pl.ds(pl.multiple_of(qi * tq, tq), tq)
        o_ref[rows, :] = ot_ref[qi].T.astype(BF16)
        return carry

    lax.fori_loop(0, nq, emit, 0)


def _attn_plain_body(qt_ref, k_ref, vt_ref, o_ref):
    nq, _, tq = qt_ref.shape
    n_kv = vt_ref.shape[0]
    ones = jnp.ones((SUM_ROWS, PLAIN_KEYS), BF16)

    def q_tile(qi, carry):
        n_steps = n_kv * KV_CHUNK // PLAIN_KEYS

        def scores(step):
            rows = slice(step * PLAIN_KEYS, (step + 1) * PLAIN_KEYS)
            return [_dot(k_ref[rows, hh * HEAD_PAD:(hh + 1) * HEAD_PAD],
                         qt_ref[qi, hh * HEAD_PAD:(hh + 1) * HEAD_PAD, :])
                    for hh in range(2)]

        acc = [None, None]
        st = scores(0)
        for step in range(n_steps):
            st_next = scores(step + 1) if step + 1 < n_steps else None
            j, part = divmod(step * PLAIN_KEYS, KV_CHUNK)
            for hh in range(2):
                vt = jnp.concatenate(
                    [vt_ref[j, hh * V_HEAD:(hh + 1) * V_HEAD, part:part + PLAIN_KEYS], ones], axis=0)
                pv = _dot(vt, jnp.exp2(st[hh]).astype(BF16))
                acc[hh] = pv if acc[hh] is None else acc[hh] + pv
            st = st_next
        out = [a[:V_HEAD] / a[V_HEAD:V_HEAD + 1] for a in acc]
        rows = pl.ds(pl.multiple_of(qi * tq, tq), tq)
        o_ref[rows, :] = jnp.concatenate(out, axis=0).T.astype(BF16)
        return carry

    lax.fori_loop(0, nq, q_tile, 0)


def _attn(qt, k, vt, batch, seq, plain_ok):
    n = k.shape[0]
    nq = seq // Q_TILE
    n_kv = seq // KV_CHUNK
    acc = pltpu.VMEM((2, V_HEAD + SUM_ROWS, Q_TILE), F32)

    def call(body, scratch, name):
        return pl.pallas_call(
            body,
            grid=(batch, MLA_HEADS // 2),
            in_specs=[
                pl.BlockSpec((nq, 2 * HEAD_PAD, Q_TILE), lambda b, hp: (b, hp, 0)),
                pl.BlockSpec((seq, 2 * HEAD_PAD), lambda b, hp: (b, hp)),
                pl.BlockSpec((n_kv, 2 * V_HEAD, KV_CHUNK), lambda b, hp: (b, hp, 0)),
            ],
            out_specs=pl.BlockSpec((seq, 2 * V_HEAD), lambda b, hp: (b, hp)),
            out_shape=jax.ShapeDtypeStruct((n, MLA_HEADS * V_HEAD), BF16),
            scratch_shapes=scratch,
            compiler_params=_params(2),
            name=name,
        )

    flash = call(_attn_flash_body,
                 [pltpu.VMEM((2, 4, KV_CHUNK, Q_TILE), F32), acc,
                  pltpu.VMEM((nq, 2 * V_HEAD, Q_TILE), F32)], "attn_flash")
    plain = call(_attn_plain_body, [], "attn_plain")
    return lax.cond(plain_ok, plain, flash, qt, k, vt)


def _memkv_body(mem_ref, g_ref, w_ref, gk_ref, k_ref, v_ref):
    m = _rms(mem_ref[...], g_ref[...]).astype(BF16)
    kv = _dot(m, w_ref[...])
    width = XA_HEADS * XA_HEAD
    for hd in range(XA_HEADS):
        sl = slice(hd * XA_HEAD, (hd + 1) * XA_HEAD)
        k_ref[:, sl] = _rms(kv[:, sl], gk_ref[...]).astype(BF16)
    v_ref[...] = kv[:, width:].astype(BF16)


def _memkv(mem, w):
    n = mem.shape[0]
    width = XA_HEADS * XA_HEAD
    blk = lambda cols: pl.BlockSpec((N_MEM, cols), lambda b: (b, 0))
    consts = [w["g_mem"], w["w_mem_kv"], w["g_xk"]]
    return pl.pallas_call(
        _memkv_body,
        grid=(n // N_MEM,),
        in_specs=[blk(D_MODEL)] + [_const_spec(a.shape) for a in consts],
        out_specs=[blk(width), blk(width)],
        out_shape=[jax.ShapeDtypeStruct((n, width), BF16)] * 2,
        compiler_params=_params(1),
        name="memkv",
    )(mem, *consts)


def _mix_body(x_ref, xprev_ref, xnext_ref, o_ref, mk_ref, mv_ref, gmix_ref, win_ref,
              convw_ref, gxq_ref, womla_ref, woconv_ref, womem_ref,
              wout_ref, y_ref, u_ref, *, tiles_per_seq):
    i = pl.program_id(0)
    tm = x_ref.shape[0]
    x = x_ref[...]
    h = _rms(x, gmix_ref[...]).astype(BF16)

    o_cc, o_cx, o_xq = CONV_WIDTH, 2 * CONV_WIDTH, 3 * CONV_WIDTH
    o_gate = o_xq + XA_HEADS * XA_HEAD
    proj = lambda hh, start, width: _dot(hh, win_ref[:, start:start + width])

    def gated(hh):
        return proj(hh, o_cc, CONV_WIDTH) * proj(hh, o_cx, CONV_WIDTH)

    first = (i % tiles_per_seq) == 0
    last = (i % tiles_per_seq) == tiles_per_seq - 1
    h_prev = _rms(xprev_ref[...], gmix_ref[...]).astype(BF16)
    h_next = _rms(xnext_ref[...], gmix_ref[...]).astype(BF16)
    u_ref[0:8, :] = jnp.where(first, 0.0, gated(h_prev))
    u_ref[8:8 + tm, :] = gated(h)
    u_ref[8 + tm:16 + tm, :] = jnp.where(last, 0.0, gated(h_next))
    cw = convw_ref[...]
    conv = (u_ref[7:7 + tm, :] * cw[0:1, :] + u_ref[8:8 + tm, :] * cw[1:2, :]
            + u_ref[9:9 + tm, :] * cw[2:3, :])
    y_conv = _dot((proj(h, 0, CONV_WIDTH) * conv).astype(BF16), woconv_ref[...])

    xq = proj(h, o_xq, XA_HEADS * XA_HEAD)
    heads = []
    for hd in range(XA_HEADS):
        sl = slice(hd * XA_HEAD, (hd + 1) * XA_HEAD)
        qh = (_rms(xq[:, sl], gxq_ref[...]) * (XA_HEAD ** -0.5)).astype(BF16)
        s = _dot_nt(qh, mk_ref[:, sl])
        p = jnp.exp(s - jnp.max(s, axis=-1, keepdims=True))
        l = jnp.sum(p, axis=-1, keepdims=True)
        heads.append((_dot(p.astype(BF16), mv_ref[:, sl]) / l).astype(BF16))
    y_mem = _dot(jnp.concatenate(heads, axis=-1), womem_ref[...])

    y_mla = _dot(o_ref[...], womla_ref[...])

    gate = lambda b: jax.nn.sigmoid(proj(h, o_gate + b * D_MODEL, D_MODEL))
    merged = gate(0) * y_mla + gate(1) * y_conv + gate(2) * y_mem
    y_ref[...] = x + _dot(merged.astype(BF16), wout_ref[...])


def _mix(x1, o, mk, mv, seq, w, tm):
    n = x1.shape[0]
    tiles_per_seq = seq // tm
    halo = tm // 8
    n_halo = n // 8
    tok = lambda width: pl.BlockSpec((tm, width), lambda i: (i, 0))
    mem = pl.BlockSpec((N_MEM, XA_HEADS * XA_HEAD), lambda i: (i // tiles_per_seq, 0))
    consts = [w["g_mix"], w["w_in_mix"], w["conv_w"],
              w["g_xq"], w["w_o_mla"], w["w_o_conv"], w["w_o_mem"], w["w_out"]]
    return pl.pallas_call(
        functools.partial(_mix_body, tiles_per_seq=tiles_per_seq),
        grid=(n // tm,),
        in_specs=[
            tok(D_MODEL),
            pl.BlockSpec((8, D_MODEL), lambda i: (jnp.maximum(i * halo - 1, 0), 0)),
            pl.BlockSpec((8, D_MODEL), lambda i: (jnp.minimum((i + 1) * halo, n_halo - 1), 0)),
            tok(MLA_HEADS * V_HEAD), mem, mem,
        ] + [_const_spec(a.shape) for a in consts],
        out_specs=tok(D_MODEL),
        out_shape=jax.ShapeDtypeStruct((n, D_MODEL), F32),
        scratch_shapes=[pltpu.VMEM((tm + 16, CONV_WIDTH), F32)],
        compiler_params=_params(1),
        name="mix",
    )(x1, x1, x1, o, mk, mv, *consts)


def _pad_heads(w, heads, width):
    k = w.shape[0]
    w = w.reshape(k, heads, width)
    return jnp.pad(w, ((0, 0), (0, 0), (0, HEAD_PAD - width))).reshape(k, heads * HEAD_PAD)


def _ffn_weights(norm, w_gu, w_down):
    return norm.reshape(1, D_MODEL), w_gu.astype(BF16), w_down.astype(BF16)


def _logit_bound(gain_q, gain_k):
    return 1.02 * (QK_HEAD ** 0.5) * LOG2_E * jnp.max(jnp.abs(gain_q)) * jnp.max(jnp.abs(gain_k))


def _rope_tables(seq):
    half = QK_ROPE // 2
    inv_freq = ROPE_BASE ** (-jnp.arange(half, dtype=F32) / half)
    ang = jnp.arange(seq, dtype=jnp.int32).astype(F32)[:, None] * inv_freq[None, :]
    cos, sin = jnp.cos(ang), jnp.sin(ang)
    zeros = lambda width: jnp.zeros((seq, width), F32)
    tail = HEAD_PAD - QK_HEAD
    cos_t = jnp.concatenate([jnp.ones((seq, QK_NOPE), F32), cos, cos, zeros(tail)], axis=1)
    sin_t = jnp.concatenate([zeros(QK_NOPE), -sin, sin, zeros(tail)], axis=1)
    return cos_t, sin_t, cos.T, sin.T


def _trunk(x, mem, w, ffn1, ffn2, tables):
    batch, seq, _ = x.shape
    x = x.reshape(batch * seq, D_MODEL)
    x1 = _ffn(x, *ffn1, tm=512)
    qt, k, vt = _qkv(x1, seq, w, tables, tm=512)
    o = _attn(qt, k, vt, batch, seq, w["plain_ok"])
    mk, mv = _memkv(mem.reshape(batch * N_MEM, D_MODEL), w)
    x2 = _mix(x1, o, mk, mv, seq, w, tm=512)
    y = _ffn(x2, *ffn2, tm=512)
    return y.reshape(batch, seq, D_MODEL)


def kernel(x_prompt, x_sample, mem_prompt, mem_sample, ffn1_norm, ffn1_w_gu, ffn1_w_down, mix_norm, w_in, q_lora_norm, w_uq, kv_lora_norm, w_uk, w_uv, mla_q_norm, mla_k_norm, w_o_mla, conv_w, w_o_conv, mem_norm, w_mem_kv, xa_q_norm, xa_k_norm, w_o_mem, w_out, ffn2_norm, ffn2_w_gu, ffn2_w_down):
    ffn1 = _ffn_weights(ffn1_norm[0], ffn1_w_gu[0], ffn1_w_down[0])
    ffn2 = _ffn_weights(ffn2_norm[0], ffn2_w_gu[0], ffn2_w_down[0])

    w_in0 = w_in[0]
    o_cq, o_ckv, o_kr = 0, Q_LORA, Q_LORA + KV_LORA
    o_cb = o_kr + QK_ROPE
    cols = lambda start, width: w_in0[:, start:start + width]
    half = QK_ROPE // 2
    swap_halves = lambda a: jnp.concatenate([a[..., half:], a[..., :half]], axis=-1)
    rope_lanes = lambda a: jnp.pad(a, ((0, 0), (QK_NOPE, HEAD_PAD - QK_HEAD)))
    kr_pad = rope_lanes(cols(o_kr, QK_ROPE))
    kr_swap_pad = rope_lanes(swap_halves(cols(o_kr, QK_ROPE)))
    gk_swap = rope_lanes(swap_halves(mla_k_norm[0][QK_NOPE:]).reshape(1, QK_ROPE))
    row = lambda g: g.reshape(1, -1)
    pad_gain = lambda g: jnp.pad(g, (0, HEAD_PAD - QK_HEAD))
    w = {
        "g_mix": row(mix_norm[0]),
        "w_a": jnp.concatenate([cols(o_cq, Q_LORA), cols(o_ckv, KV_LORA), kr_pad, kr_swap_pad],
                               axis=1).astype(BF16),
        "g_q": row(q_lora_norm[0]),
        "w_uq_t": _pad_heads(w_uq[0], MLA_HEADS, QK_HEAD).T.astype(BF16),
        "g_kv": row(kv_lora_norm[0]),
        "w_uk": _pad_heads(w_uk[0], MLA_HEADS, QK_NOPE).astype(BF16),
        "w_uv_t": w_uv[0].T.astype(BF16),
        "g_qh_col": pad_gain(mla_q_norm[0]).reshape(HEAD_PAD, 1),
        "g_kh": pad_gain(mla_k_norm[0]).reshape(1, HEAD_PAD),
        "g_kh_swap": gk_swap,
        "w_in_mix": w_in0[:, o_cb:].astype(BF16),
        "conv_w": conv_w[0],
        "g_xq": row(xa_q_norm[0]),
        "w_o_mla": w_o_mla[0].astype(BF16),
        "w_o_conv": w_o_conv[0].astype(BF16),
        "w_o_mem": w_o_mem[0].astype(BF16),
        "w_out": w_out[0].astype(BF16),
        "g_mem": row(mem_norm[0]),
        "w_mem_kv": w_mem_kv[0].astype(BF16),
        "g_xk": row(xa_k_norm[0]),
        "plain_ok": _logit_bound(mla_q_norm[0], mla_k_norm[0]) <= MAX_PLAIN_LOGIT,
    }
    tables = _rope_tables(max(x_prompt.shape[1], x_sample.shape[1]))
    y_prompt = _trunk(x_prompt, mem_prompt, w, ffn1, ffn2, tables)
    y_sample = _trunk(x_sample, mem_sample, w, ffn1, ffn2, tables)
    return (y_prompt, y_sample)
```

```python
import functools

import jax
import jax.numpy as jnp
from jax import lax
from jax.experimental import pallas as pl
from jax.experimental.pallas import tpu as pltpu

D_MODEL = 1024
N_MEM = 256
MLA_HEADS = 8
QK_NOPE = 64
QK_ROPE = 32
QK_HEAD = QK_NOPE + QK_ROPE
V_HEAD = 64
Q_LORA = 384
KV_LORA = 256
CONV_WIDTH = 512
XA_HEADS = 4
XA_HEAD = 128
D_FF = 2816
ROPE_BASE = 10000.0
EPS = 1e-6
LOG2_E = 1.4426950408889634

LANES = 128
HEAD_PAD = LANES
FF_CHUNK = 256
N_FF_CHUNKS = D_FF // FF_CHUNK
KV_CHUNK = 512
Q_TILE = 512
SUM_ROWS = 16
PLAIN_KEYS = 256
MAX_PLAIN_LOGIT = 60.0
VMEM_LIMIT = 56 * 1024 * 1024

BF16 = jnp.bfloat16
F32 = jnp.float32


def _const_spec(shape):
    zeros = (0,) * len(shape)
    return pl.BlockSpec(shape, lambda *_: zeros, pipeline_mode=pl.Buffered(1))


def _params(n_axes):
    return pltpu.CompilerParams(
        dimension_semantics=("arbitrary",) * n_axes,
        vmem_limit_bytes=VMEM_LIMIT)


def _rms(x, gain, n=None):
    n = x.shape[-1] if n is None else n
    inv = lax.rsqrt(jnp.sum(x * x, axis=-1, keepdims=True) * (1.0 / n) + EPS)
    return (x * inv) * gain


def _dot(a, b):
    return jnp.dot(a, b, preferred_element_type=F32)


def _dot_nt(a, b):
    return lax.dot_general(a, b, (((1,), (1,)), ((), ())), preferred_element_type=F32)


def _ffn_body(x_ref, g_ref, wgu_ref, wd_ref, o_ref, xn_ref, acc_ref):
    xn_ref[...] = _rms(x_ref[...], g_ref[...]).astype(BF16)
    acc_ref[...] = jnp.zeros_like(acc_ref)
    for c in range(N_FF_CHUNKS):
        cols = slice(c * FF_CHUNK, (c + 1) * FF_CHUNK)
        up_cols = slice(D_FF + c * FF_CHUNK, D_FF + (c + 1) * FF_CHUNK)
        xn = xn_ref[...]
        g = _dot(xn, wgu_ref[:, cols])
        u = _dot(xn, wgu_ref[:, up_cols])
        a = (g * jax.nn.sigmoid(g) * u).astype(BF16)
        acc_ref[...] += _dot(a, wd_ref[cols, :])
    o_ref[...] = x_ref[...] + 0.5 * acc_ref[...]


def _ffn(x, gain, wgu, wd, tm):
    n = x.shape[0]
    return pl.pallas_call(
        _ffn_body,
        grid=(n // tm,),
        in_specs=[
            pl.BlockSpec((tm, D_MODEL), lambda i: (i, 0)),
            _const_spec((1, D_MODEL)),
            _const_spec(wgu.shape),
            _const_spec(wd.shape),
        ],
        out_specs=pl.BlockSpec((tm, D_MODEL), lambda i: (i, 0)),
        out_shape=jax.ShapeDtypeStruct((n, D_MODEL), F32),
        scratch_shapes=[pltpu.VMEM((tm, D_MODEL), BF16), pltpu.VMEM((tm, D_MODEL), F32)],
        compiler_params=_params(1),
        name="ffn",
    )(x, gain, wgu, wd)


def _qkv_body(x_ref, gmix_ref, wa_ref, gq_ref, wuqt_ref, gkv_ref, wuk_ref, wuvt_ref,
              gqh_ref, gkh_ref, gkhs_ref, cos_ref, sin_ref, cost_ref, sint_ref,
              qt_ref, k_ref, vt_ref):
    tm = x_ref.shape[0]
    half = QK_ROPE // 2
    h = _rms(x_ref[...], gmix_ref[...]).astype(BF16)
    c = _dot(h, wa_ref[...])
    cq = _rms(c[:, :Q_LORA], gq_ref[...]).astype(BF16)
    ckv = _rms(c[:, Q_LORA:Q_LORA + KV_LORA], gkv_ref[...]).astype(BF16)
    o_kr = Q_LORA + KV_LORA
    kr = c[:, o_kr:o_kr + HEAD_PAD]
    kr_swap = c[:, o_kr + HEAD_PAD:]

    vt = _dot_nt(wuvt_ref[...], ckv).astype(BF16)
    for ch in range(tm // KV_CHUNK):
        vt_ref[ch] = vt[:, ch * KV_CHUNK:(ch + 1) * KV_CHUNK]

    kn = _dot(ckv, wuk_ref[...])
    cos, sin = cos_ref[...], sin_ref[...]
    for hd in range(MLA_HEADS):
        sl = slice(hd * HEAD_PAD, (hd + 1) * HEAD_PAD)
        kh = kn[:, sl] + kr
        inv = lax.rsqrt(jnp.sum(kh * kh, axis=-1, keepdims=True) * (1.0 / QK_HEAD) + EPS)
        rot = ((kh * inv) * gkh_ref[...]) * cos + ((kr_swap * inv) * gkhs_ref[...]) * sin
        k_ref[:, sl] = rot.astype(BF16)

    qt = _dot_nt(wuqt_ref[...], cq)
    cost, sint = cost_ref[...], sint_ref[...]
    scale = QK_HEAD ** -0.5 * LOG2_E

    def put(row, rows_f32):
        for t in range(tm // Q_TILE):
            qt_ref[t, row:row + rows_f32.shape[0], :] = (
                rows_f32[:, t * Q_TILE:(t + 1) * Q_TILE].astype(BF16))

    for hd in range(MLA_HEADS):
        r0 = hd * HEAD_PAD
        blk = qt[r0:r0 + HEAD_PAD, :]
        inv = lax.rsqrt(jnp.sum(blk * blk, axis=0, keepdims=True) * (1.0 / QK_HEAD) + EPS)
        qn = (blk * inv) * gqh_ref[...]
        x1, x2 = qn[QK_NOPE:QK_NOPE + half], qn[QK_NOPE + half:QK_HEAD]
        put(r0, qn[:QK_NOPE] * scale)
        put(r0 + QK_NOPE, (x1 * cost - x2 * sint) * scale)
        put(r0 + QK_NOPE + half, (x2 * cost + x1 * sint) * scale)
        put(r0 + QK_HEAD, jnp.zeros((HEAD_PAD - QK_HEAD, tm), F32))


def _qkv(x1, seq, w, tables, tm):
    n = x1.shape[0]
    tiles_per_seq = seq // tm
    tok = lambda width: pl.BlockSpec((tm, width), lambda i: (i, 0))
    tab = pl.BlockSpec((tm, HEAD_PAD), lambda i: (i % tiles_per_seq, 0))
    tab_t = pl.BlockSpec((QK_ROPE // 2, tm), lambda i: (0, i % tiles_per_seq))
    consts = [w["g_mix"], w["w_a"], w["g_q"], w["w_uq_t"], w["g_kv"], w["w_uk"], w["w_uv_t"],
              w["g_qh_col"], w["g_kh"], w["g_kh_swap"]]
    return pl.pallas_call(
        _qkv_body,
        grid=(n // tm,),
        in_specs=([tok(D_MODEL)] + [_const_spec(a.shape) for a in consts]
                  + [tab, tab, tab_t, tab_t]),
        out_specs=[
            pl.BlockSpec((tm // Q_TILE, MLA_HEADS * HEAD_PAD, Q_TILE), lambda i: (i, 0, 0)),
            tok(MLA_HEADS * HEAD_PAD),
            pl.BlockSpec((tm // KV_CHUNK, MLA_HEADS * V_HEAD, KV_CHUNK), lambda i: (i, 0, 0)),
        ],
        out_shape=[jax.ShapeDtypeStruct((n // Q_TILE, MLA_HEADS * HEAD_PAD, Q_TILE), BF16),
                   jax.ShapeDtypeStruct((n, MLA_HEADS * HEAD_PAD), BF16),
                   jax.ShapeDtypeStruct((n // KV_CHUNK, MLA_HEADS * V_HEAD, KV_CHUNK), BF16)],
        compiler_params=_params(1),
        name="qkv",
    )(x1, *consts, *tables)


def _attn_flash_body(qt_ref, k_ref, vt_ref, o_ref, s_ref, acc_ref, ot_ref):
    nq, _, tq = qt_ref.shape
    n_kv = vt_ref.shape[0]
    ones = jnp.ones((SUM_ROWS, KV_CHUNK), BF16)
    m0 = jnp.full((1, tq), -jnp.inf, F32)
    heads = range(2)

    def scores(qi, j, slot):
        rows = pl.ds(pl.multiple_of(j * KV_CHUNK, KV_CHUNK), KV_CHUNK)
        cmax = []
        for hh in heads:
            st = _dot(k_ref[rows, hh * HEAD_PAD:(hh + 1) * HEAD_PAD],
                      qt_ref[qi, hh * HEAD_PAD:(hh + 1) * HEAD_PAD, :])
            s_ref[hh, slot] = st
            cmax.append(jnp.max(st, axis=0, keepdims=True))
        return tuple(cmax)

    def consume(j, slot, ms, cmax):
        out = []
        for hh in heads:
            m_new = jnp.maximum(ms[hh], cmax[hh])
            alpha = jnp.exp2(ms[hh] - m_new)
            pt = jnp.exp2(s_ref[hh, slot] - m_new).astype(BF16)
            vt = jnp.concatenate([vt_ref[j, hh * V_HEAD:(hh + 1) * V_HEAD, :], ones], axis=0)
            acc_ref[hh] = alpha * acc_ref[hh] + _dot(vt, pt)
            out.append(m_new)
        return tuple(out)

    def finish(qi):
        for hh in heads:
            ot_ref[qi, hh * V_HEAD:(hh + 1) * V_HEAD, :] = (
                acc_ref[hh, :V_HEAD, :] / acc_ref[hh, V_HEAD:V_HEAD + 1, :])
        acc_ref[...] = jnp.zeros_like(acc_ref)

    def q_tile(qi, carry):
        def group(g, carry):
            ms, c0, c1 = carry
            j = 4 * g
            c2 = scores(qi, j + 2, 2)
            ms = consume(j, 0, ms, c0)
            c3 = scores(qi, j + 3, 3)
            ms = consume(j + 1, 1, ms, c1)
            c0 = scores(qi, j + 4, 0)
            ms = consume(j + 2, 2, ms, c2)
            c1 = scores(qi, j + 5, 1)
            ms = consume(j + 3, 3, ms, c3)
            return ms, c0, c1

        ms, c0, c1 = lax.fori_loop(0, n_kv // 4 - 1, group, ((m0, m0),) + carry, unroll=True)
        q_next = jnp.minimum(qi + 1, nq - 1)
        c2 = scores(qi, n_kv - 2, 2)
        ms = consume(n_kv - 4, 0, ms, c0)
        c3 = scores(qi, n_kv - 1, 3)
        ms = consume(n_kv - 3, 1, ms, c1)
        c0 = scores(q_next, 0, 0)
        ms = consume(n_kv - 2, 2, ms, c2)
        c1 = scores(q_next, 1, 1)
        consume(n_kv - 1, 3, ms, c3)
        finish(qi)
        return c0, c1

    acc_ref[...] = jnp.zeros_like(acc_ref)
    lax.fori_loop(0, nq, q_tile, (scores(0, 0, 0), scores(0, 1, 1)), unroll=2)

    def emit(qi, carry):
        rows = pl.ds(pl.multiple_of(qi * tq, tq), tq)
        o_ref[rows, :] = ot_ref[qi].T.astype(BF16)
        return carry

    lax.fori_loop(0, nq, emit, 0)


def _attn_plain_body(qt_ref, k_ref, vt_ref, o_ref):
    nq, _, tq = qt_ref.shape
    n_kv = vt_ref.shape[0]
    ones = jnp.ones((SUM_ROWS, PLAIN_KEYS), BF16)

    def q_tile(qi, carry):
        n_steps = n_kv * KV_CHUNK // PLAIN_KEYS

        def scores(step, hh):
            rows = slice(step * PLAIN_KEYS, (step + 1) * PLAIN_KEYS)
            return _dot(k_ref[rows, hh * HEAD_PAD:(hh + 1) * HEAD_PAD],
                        qt_ref[qi, hh * HEAD_PAD:(hh + 1) * HEAD_PAD, :])

        acc = [None, None]
        st = [scores(0, 0), scores(0, 1)]
        for step in range(n_steps):
            st_next = [None, None]
            j, part = divmod(step * PLAIN_KEYS, KV_CHUNK)
            for hh in range(2):
                if step + 1 < n_steps:
                    st_next[hh] = scores(step + 1, hh)
                vt = jnp.concatenate(
                    [vt_ref[j, hh * V_HEAD:(hh + 1) * V_HEAD, part:part + PLAIN_KEYS], ones], axis=0)
                pv = _dot(vt, jnp.exp2(st[hh]).astype(BF16))
                acc[hh] = pv if acc[hh] is None else acc[hh] + pv
            st = st_next
        out = [a[:V_HEAD] / a[V_HEAD:V_HEAD + 1] for a in acc]
        rows = pl.ds(pl.multiple_of(qi * tq, tq), tq)
        o_ref[rows, :] = jnp.concatenate(out, axis=0).T.astype(BF16)
        return carry

    lax.fori_loop(0, nq, q_tile, 0)


def _attn(qt, k, vt, batch, seq, plain_ok):
    n = k.shape[0]
    nq = seq // Q_TILE
    n_kv = seq // KV_CHUNK
    acc = pltpu.VMEM((2, V_HEAD + SUM_ROWS, Q_TILE), F32)

    def call(body, scratch, name):
        return pl.pallas_call(
            body,
            grid=(batch, MLA_HEADS // 2),
            in_specs=[
                pl.BlockSpec((nq, 2 * HEAD_PAD, Q_TILE), lambda b, hp: (b, hp, 0)),
                pl.BlockSpec((seq, 2 * HEAD_PAD), lambda b, hp: (b, hp)),
                pl.BlockSpec((n_kv, 2 * V_HEAD, KV_CHUNK), lambda b, hp: (b, hp, 0)),
            ],
            out_specs=pl.BlockSpec((seq, 2 * V_HEAD), lambda b, hp: (b, hp)),
            out_shape=jax.ShapeDtypeStruct((n, MLA_HEADS * V_HEAD), BF16),
            scratch_shapes=scratch,
            compiler_params=_params(2),
            name=name,
        )

    flash = call(_attn_flash_body,
                 [pltpu.VMEM((2, 4, KV_CHUNK, Q_TILE), F32), acc,
                  pltpu.VMEM((nq, 2 * V_HEAD, Q_TILE), F32)], "attn_flash")
    plain = call(_attn_plain_body, [], "attn_plain")
    return lax.cond(plain_ok, plain, flash, qt, k, vt)


def _memkv_body(mem_ref, g_ref, w_ref, gk_ref, k_ref, v_ref):
    m = _rms(mem_ref[...], g_ref[...]).astype(BF16)
    kv = _dot(m, w_ref[...])
    width = XA_HEADS * XA_HEAD
    for hd in range(XA_HEADS):
        sl = slice(hd * XA_HEAD, (hd + 1) * XA_HEAD)
        k_ref[:, sl] = _rms(kv[:, sl], gk_ref[...]).astype(BF16)
    v_ref[...] = kv[:, width:].astype(BF16)


def _memkv(mem, w):
    n = mem.shape[0]
    width = XA_HEADS * XA_HEAD
    blk = lambda cols: pl.BlockSpec((N_MEM, cols), lambda b: (b, 0))
    consts = [w["g_mem"], w["w_mem_kv"], w["g_xk"]]
    return pl.pallas_call(
        _memkv_body,
        grid=(n // N_MEM,),
        in_specs=[blk(D_MODEL)] + [_const_spec(a.shape) for a in consts],
        out_specs=[blk(width), blk(width)],
        out_shape=[jax.ShapeDtypeStruct((n, width), BF16)] * 2,
        compiler_params=_params(1),
        name="memkv",
    )(mem, *consts)


def _mix_body(x_ref, xprev_ref, xnext_ref, o_ref, mk_ref, mv_ref, gmix_ref, win_ref,
              convw_ref, gxq_ref, womla_ref, woconv_ref, womem_ref,
              wout_ref, y_ref, u_ref, *, tiles_per_seq):
    i = pl.program_id(0)
    tm = x_ref.shape[0]
    x = x_ref[...]
    h = _rms(x, gmix_ref[...]).astype(BF16)

    o_cc, o_cx, o_xq = CONV_WIDTH, 2 * CONV_WIDTH, 3 * CONV_WIDTH
    o_gate = o_xq + XA_HEADS * XA_HEAD
    proj = lambda hh, start, width: _dot(hh, win_ref[:, start:start + width])

    def gated(hh):
        return proj(hh, o_cc, CONV_WIDTH) * proj(hh, o_cx, CONV_WIDTH)

    first = (i % tiles_per_seq) == 0
    last = (i % tiles_per_seq) == tiles_per_seq - 1
    h_prev = _rms(xprev_ref[...], gmix_ref[...]).astype(BF16)
    h_next = _rms(xnext_ref[...], gmix_ref[...]).astype(BF16)
    u_ref[0:8, :] = jnp.where(first, 0.0, gated(h_prev))
    u_ref[8:8 + tm, :] = gated(h)
    u_ref[8 + tm:16 + tm, :] = jnp.where(last, 0.0, gated(h_next))
    cw = convw_ref[...]
    conv = (u_ref[7:7 + tm, :] * cw[0:1, :] + u_ref[8:8 + tm, :] * cw[1:2, :]
            + u_ref[9:9 + tm, :] * cw[2:3, :])
    y_conv = _dot((proj(h, 0, CONV_WIDTH) * conv).astype(BF16), woconv_ref[...])

    xq = proj(h, o_xq, XA_HEADS * XA_HEAD)
    heads = []
    for hd in range(XA_HEADS):
        sl = slice(hd * XA_HEAD, (hd + 1) * XA_HEAD)
        qh = (_rms(xq[:, sl], gxq_ref[...]) * (XA_HEAD ** -0.5)).astype(BF16)
        s = _dot_nt(qh, mk_ref[:, sl])
        p = jnp.exp(s - jnp.max(s, axis=-1, keepdims=True))
        l = jnp.sum(p, axis=-1, keepdims=True)
        heads.append((_dot(p.astype(BF16), mv_ref[:, sl]) / l).astype(BF16))
    y_mem = _dot(jnp.concatenate(heads, axis=-1), womem_ref[...])

    y_mla = _dot(o_ref[...], womla_ref[...])

    gate = lambda b: jax.nn.sigmoid(proj(h, o_gate + b * D_MODEL, D_MODEL))
    merged = gate(0) * y_mla + gate(1) * y_conv + gate(2) * y_mem
    y_ref[...] = x + _dot(merged.astype(BF16), wout_ref[...])


def _mix(x1, o, mk, mv, seq, w, tm):
    n = x1.shape[0]
    tiles_per_seq = seq // tm
    halo = tm // 8
    n_halo = n // 8
    tok = lambda width: pl.BlockSpec((tm, width), lambda i: (i, 0))
    mem = pl.BlockSpec((N_MEM, XA_HEADS * XA_HEAD), lambda i: (i // tiles_per_seq, 0))
    consts = [w["g_mix"], w["w_in_mix"], w["conv_w"],
              w["g_xq"], w["w_o_mla"], w["w_o_conv"], w["w_o_mem"], w["w_out"]]
    return pl.pallas_call(
        functools.partial(_mix_body, tiles_per_seq=tiles_per_seq),
        grid=(n // tm,),
        in_specs=[
            tok(D_MODEL),
            pl.BlockSpec((8, D_MODEL), lambda i: (jnp.maximum(i * halo - 1, 0), 0)),
            pl.BlockSpec((8, D_MODEL), lambda i: (jnp.minimum((i + 1) * halo, n_halo - 1), 0)),
            tok(MLA_HEADS * V_HEAD), mem, mem,
        ] + [_const_spec(a.shape) for a in consts],
        out_specs=tok(D_MODEL),
        out_shape=jax.ShapeDtypeStruct((n, D_MODEL), F32),
        scratch_shapes=[pltpu.VMEM((tm + 16, CONV_WIDTH), F32)],
        compiler_params=_params(1),
        name="mix",
    )(x1, x1, x1, o, mk, mv, *consts)


def _pad_heads(w, heads, width):
    k = w.shape[0]
    w = w.reshape(k, heads, width)
    return jnp.pad(w, ((0, 0), (0, 0), (0, HEAD_PAD - width))).reshape(k, heads * HEAD_PAD)


def _ffn_weights(norm, w_gu, w_down):
    return norm.reshape(1, D_MODEL), w_gu.astype(BF16), w_down.astype(BF16)


def _logit_bound(gain_q, gain_k):
    return 1.02 * (QK_HEAD ** 0.5) * LOG2_E * jnp.max(jnp.abs(gain_q)) * jnp.max(jnp.abs(gain_k))


def _rope_tables(seq):
    half = QK_ROPE // 2
    inv_freq = ROPE_BASE ** (-jnp.arange(half, dtype=F32) / half)
    ang = jnp.arange(seq, dtype=jnp.int32).astype(F32)[:, None] * inv_freq[None, :]
    cos, sin = jnp.cos(ang), jnp.sin(ang)
    zeros = lambda width: jnp.zeros((seq, width), F32)
    tail = HEAD_PAD - QK_HEAD
    cos_t = jnp.concatenate([jnp.ones((seq, QK_NOPE), F32), cos, cos, zeros(tail)], axis=1)
    sin_t = jnp.concatenate([zeros(QK_NOPE), -sin, sin, zeros(tail)], axis=1)
    return cos_t, sin_t, cos.T, sin.T


def _trunk(x, mem, w, ffn1, ffn2, tables):
    batch, seq, _ = x.shape
    x = x.reshape(batch * seq, D_MODEL)
    x1 = _ffn(x, *ffn1, tm=512)
    qt, k, vt = _qkv(x1, seq, w, tables, tm=512)
    o = _attn(qt, k, vt, batch, seq, w["plain_ok"])
    mk, mv = _memkv(mem.reshape(batch * N_MEM, D_MODEL), w)
    x2 = _mix(x1, o, mk, mv, seq, w, tm=512)
    y = _ffn(x2, *ffn2, tm=512)
    return y.reshape(batch, seq, D_MODEL)


def kernel(x_prompt, x_sample, mem_prompt, mem_sample, ffn1_norm, ffn1_w_gu, ffn1_w_down, mix_norm, w_in, q_lora_norm, w_uq, kv_lora_norm, w_uk, w_uv, mla_q_norm, mla_k_norm, w_o_mla, conv_w, w_o_conv, mem_norm, w_mem_kv, xa_q_norm, xa_k_norm, w_o_mem, w_out, ffn2_norm, ffn2_w_gu, ffn2_w_down):
    ffn1 = _ffn_weights(ffn1_norm[0], ffn1_w_gu[0], ffn1_w_down[0])
    ffn2 = _ffn_weights(ffn2_norm[0], ffn2_w_gu[0], ffn2_w_down[0])

    w_in0 = w_in[0]
    o_cq, o_ckv, o_kr = 0, Q_LORA, Q_LORA + KV_LORA
    o_cb = o_kr + QK_ROPE
    cols = lambda start, width: w_in0[:, start:start + width]
    half = QK_ROPE // 2
    swap_halves = lambda a: jnp.concatenate([a[..., half:], a[..., :half]], axis=-1)
    rope_lanes = lambda a: jnp.pad(a, ((0, 0), (QK_NOPE, HEAD_PAD - QK_HEAD)))
    kr_pad = rope_lanes(cols(o_kr, QK_ROPE))
    kr_swap_pad = rope_lanes(swap_halves(cols(o_kr, QK_ROPE)))
    gk_swap = rope_lanes(swap_halves(mla_k_norm[0][QK_NOPE:]).reshape(1, QK_ROPE))
    row = lambda g: g.reshape(1, -1)
    pad_gain = lambda g: jnp.pad(g, (0, HEAD_PAD - QK_HEAD))
    w = {
        "g_mix": row(mix_norm[0]),
        "w_a": jnp.concatenate([cols(o_cq, Q_LORA), cols(o_ckv, KV_LORA), kr_pad, kr_swap_pad],
                               axis=1).astype(BF16),
        "g_q": row(q_lora_norm[0]),
        "w_uq_t": _pad_heads(w_uq[0], MLA_HEADS, QK_HEAD).T.astype(BF16),
        "g_kv": row(kv_lora_norm[0]),
        "w_uk": _pad_heads(w_uk[0], MLA_HEADS, QK_NOPE).astype(BF16),
        "w_uv_t": w_uv[0].T.astype(BF16),
        "g_qh_col": pad_gain(mla_q_norm[0]).reshape(HEAD_PAD, 1),
        "g_kh": pad_gain(mla_k_norm[0]).reshape(1, HEAD_PAD),
        "g_kh_swap": gk_swap,
        "w_in_mix": w_in0[:, o_cb:].astype(BF16),
        "conv_w": conv_w[0],
        "g_xq": row(xa_q_norm[0]),
        "w_o_mla": w_o_mla[0].astype(BF16),
        "w_o_conv": w_o_conv[0].astype(BF16),
        "w_o_mem": w_o_mem[0].astype(BF16),
        "w_out": w_out[0].astype(BF16),
        "g_mem": row(mem_norm[0]),
        "w_mem_kv": w_mem_kv[0].astype(BF16),
        "g_xk": row(xa_k_norm[0]),
        "plain_ok": _logit_bound(mla_q_norm[0], mla_k_norm[0]) <= MAX_PLAIN_LOGIT,
    }
    tables = _rope_tables(max(x_prompt.shape[1], x_sample.shape[1]))
    y_prompt = _trunk(x_prompt, mem_prompt, w, ffn1, ffn2, tables)
    y_sample = _trunk(x_sample, mem_sample, w, ffn1, ffn2, tables)
    return (y_prompt, y_sample)
```

```python
import functools

import jax
import jax.numpy as jnp
from jax import lax
from jax.experimental import pallas as pl
from jax.experimental.pallas import tpu as pltpu

D_MODEL = 1024
N_MEM = 256
MLA_HEADS = 8
QK_NOPE = 64
QK_ROPE = 32
QK_HEAD = QK_NOPE + QK_ROPE
V_HEAD = 64
Q_LORA = 384
KV_LORA = 256
CONV_WIDTH = 512
XA_HEADS = 4
XA_HEAD = 128
D_FF = 2816
ROPE_BASE = 10000.0
EPS = 1e-6
LOG2_E = 1.4426950408889634

LANES = 128
HEAD_PAD = LANES
FF_CHUNK = 256
N_FF_CHUNKS = D_FF // FF_CHUNK
KV_CHUNK = 512
Q_TILE = 512
SUM_ROWS = 16
PLAIN_KEYS = 256
MAX_PLAIN_LOGIT = 60.0
VMEM_LIMIT = 56 * 1024 * 1024

BF16 = jnp.bfloat16
F32 = jnp.float32


def _const_spec(shape):
    zeros = (0,) * len(shape)
    return pl.BlockSpec(shape, lambda *_: zeros, pipeline_mode=pl.Buffered(1))


def _params(n_axes):
    return pltpu.CompilerParams(
        dimension_semantics=("arbitrary",) * n_axes,
        vmem_limit_bytes=VMEM_LIMIT)


def _rms(x, gain, n=None):
    n = x.shape[-1] if n is None else n
    inv = lax.rsqrt(jnp.sum(x * x, axis=-1, keepdims=True) * (1.0 / n) + EPS)
    return (x * inv) * gain


def _dot(a, b):
    return jnp.dot(a, b, preferred_element_type=F32)


def _dot_nt(a, b):
    return lax.dot_general(a, b, (((1,), (1,)), ((), ())), preferred_element_type=F32)


def _ffn_body(x_ref, g_ref, wgu_ref, wd_ref, o_ref, xn_ref, acc_ref):
    xn_ref[...] = _rms(x_ref[...], g_ref[...]).astype(BF16)
    acc_ref[...] = jnp.zeros_like(acc_ref)
    for c in range(N_FF_CHUNKS):
        cols = slice(c * FF_CHUNK, (c + 1) * FF_CHUNK)
        up_cols = slice(D_FF + c * FF_CHUNK, D_FF + (c + 1) * FF_CHUNK)
        xn = xn_ref[...]
        g = _dot(xn, wgu_ref[:, cols])
        u = _dot(xn, wgu_ref[:, up_cols])
        a = (g * jax.nn.sigmoid(g) * u).astype(BF16)
        acc_ref[...] += _dot(a, wd_ref[cols, :])
    o_ref[...] = x_ref[...] + 0.5 * acc_ref[...]


def _ffn(x, gain, wgu, wd, tm):
    n = x.shape[0]
    return pl.pallas_call(
        _ffn_body,
        grid=(n // tm,),
        in_specs=[
            pl.BlockSpec((tm, D_MODEL), lambda i: (i, 0)),
            _const_spec((1, D_MODEL)),
            _const_spec(wgu.shape),
            _const_spec(wd.shape),
        ],
        out_specs=pl.BlockSpec((tm, D_MODEL), lambda i: (i, 0)),
        out_shape=jax.ShapeDtypeStruct((n, D_MODEL), F32),
        scratch_shapes=[pltpu.VMEM((tm, D_MODEL), BF16), pltpu.VMEM((tm, D_MODEL), F32)],
        compiler_params=_params(1),
        name="ffn",
    )(x, gain, wgu, wd)


def _qkv_body(x_ref, gmix_ref, wa_ref, gq_ref, wuqt_ref, gkv_ref, wuk_ref, wuvt_ref,
              gqh_ref, gkh_ref, gkhs_ref, cos_ref, sin_ref, cost_ref, sint_ref,
              qt_ref, k_ref, vt_ref):
    tm = x_ref.shape[0]
    half = QK_ROPE // 2
    h = _rms(x_ref[...], gmix_ref[...]).astype(BF16)
    c = _dot(h, wa_ref[...])
    cq = _rms(c[:, :Q_LORA], gq_ref[...]).astype(BF16)
    ckv = _rms(c[:, Q_LORA:Q_LORA + KV_LORA], gkv_ref[...]).astype(BF16)
    o_kr = Q_LORA + KV_LORA
    kr = c[:, o_kr:o_kr + HEAD_PAD]
    kr_swap = c[:, o_kr + HEAD_PAD:]

    vt = _dot_nt(wuvt_ref[...], ckv).astype(BF16)
    for ch in range(tm // KV_CHUNK):
        vt_ref[ch] = vt[:, ch * KV_CHUNK:(ch + 1) * KV_CHUNK]

    kn = _dot(ckv, wuk_ref[...])
    cos, sin = cos_ref[...], sin_ref[...]
    for hd in range(MLA_HEADS):
        sl = slice(hd * HEAD_PAD, (hd + 1) * HEAD_PAD)
        kh = kn[:, sl] + kr
        inv = lax.rsqrt(jnp.sum(kh * kh, axis=-1, keepdims=True) * (1.0 / QK_HEAD) + EPS)
        rot = ((kh * inv) * gkh_ref[...]) * cos + ((kr_swap * inv) * gkhs_ref[...]) * sin
        k_ref[:, sl] = rot.astype(BF16)

    qt = _dot_nt(wuqt_ref[...], cq)
    cost, sint = cost_ref[...], sint_ref[...]
    scale = QK_HEAD ** -0.5 * LOG2_E

    def put(row, rows_f32):
        for t in range(tm // Q_TILE):
            qt_ref[t, row:row + rows_f32.shape[0], :] = (
                rows_f32[:, t * Q_TILE:(t + 1) * Q_TILE].astype(BF16))

    for hd in range(MLA_HEADS):
        r0 = hd * HEAD_PAD
        blk = qt[r0:r0 + HEAD_PAD, :]
        inv = lax.rsqrt(jnp.sum(blk * blk, axis=0, keepdims=True) * (1.0 / QK_HEAD) + EPS)
        qn = (blk * inv) * gqh_ref[...]
        x1, x2 = qn[QK_NOPE:QK_NOPE + half], qn[QK_NOPE + half:QK_HEAD]
        put(r0, qn[:QK_NOPE] * scale)
        put(r0 + QK_NOPE, (x1 * cost - x2 * sint) * scale)
        put(r0 + QK_NOPE + half, (x2 * cost + x1 * sint) * scale)
        put(r0 + QK_HEAD, jnp.zeros((HEAD_PAD - QK_HEAD, tm), F32))


def _qkv(x1, seq, w, tables, tm):
    n = x1.shape[0]
    tiles_per_seq = seq // tm
    tok = lambda width: pl.BlockSpec((tm, width), lambda i: (i, 0))
    tab = pl.BlockSpec((tm, HEAD_PAD), lambda i: (i % tiles_per_seq, 0))
    tab_t = pl.BlockSpec((QK_ROPE // 2, tm), lambda i: (0, i % tiles_per_seq))
    consts = [w["g_mix"], w["w_a"], w["g_q"], w["w_uq_t"], w["g_kv"], w["w_uk"], w["w_uv_t"],
              w["g_qh_col"], w["g_kh"], w["g_kh_swap"]]
    return pl.pallas_call(
        _qkv_body,
        grid=(n // tm,),
        in_specs=([tok(D_MODEL)] + [_const_spec(a.shape) for a in consts]
                  + [tab, tab, tab_t, tab_t]),
        out_specs=[
            pl.BlockSpec((tm // Q_TILE, MLA_HEADS * HEAD_PAD, Q_TILE), lambda i: (i, 0, 0)),
            tok(MLA_HEADS * HEAD_PAD),
            pl.BlockSpec((tm // KV_CHUNK, MLA_HEADS * V_HEAD, KV_CHUNK), lambda i: (i, 0, 0)),
        ],
        out_shape=[jax.ShapeDtypeStruct((n // Q_TILE, MLA_HEADS * HEAD_PAD, Q_TILE), BF16),
                   jax.ShapeDtypeStruct((n, MLA_HEADS * HEAD_PAD), BF16),
                   jax.ShapeDtypeStruct((n // KV_CHUNK, MLA_HEADS * V_HEAD, KV_CHUNK), BF16)],
        compiler_params=_params(1),
        name="qkv",
    )(x1, *consts, *tables)


def _attn_flash_body(qt_ref, k_ref, vt_ref, o_ref, s_ref, acc_ref, ot_ref):
    nq, _, tq = qt_ref.shape
    n_kv = vt_ref.shape[0]
    ones = jnp.ones((SUM_ROWS, KV_CHUNK), BF16)
    m0 = jnp.full((1, tq), -jnp.inf, F32)
    heads = range(2)

    def scores(qi, j, slot):
        rows = pl.ds(pl.multiple_of(j * KV_CHUNK, KV_CHUNK), KV_CHUNK)
        cmax = []
        for hh in heads:
            st = _dot(k_ref[rows, hh * HEAD_PAD:(hh + 1) * HEAD_PAD],
                      qt_ref[qi, hh * HEAD_PAD:(hh + 1) * HEAD_PAD, :])
            s_ref[hh, slot] = st
            cmax.append(jnp.max(st, axis=0, keepdims=True))
        return tuple(cmax)

    def consume(j, slot, ms, cmax):
        out = []
        for hh in heads:
            m_new = jnp.maximum(ms[hh], cmax[hh])
            alpha = jnp.exp2(ms[hh] - m_new)
            pt = jnp.exp2(s_ref[hh, slot] - m_new).astype(BF16)
            vt = jnp.concatenate([vt_ref[j, hh * V_HEAD:(hh + 1) * V_HEAD, :], ones], axis=0)
            acc_ref[hh] = alpha * acc_ref[hh] + _dot(vt, pt)
            out.append(m_new)
        return tuple(out)

    def finish(qi):
        for hh in heads:
            ot_ref[qi, hh * V_HEAD:(hh + 1) * V_HEAD, :] = (
                acc_ref[hh, :V_HEAD, :] / acc_ref[hh, V_HEAD:V_HEAD + 1, :])
        acc_ref[...] = jnp.zeros_like(acc_ref)

    def q_tile(qi, carry):
        def group(g, carry):
            ms, c0, c1 = carry
            j = 4 * g
            c2 = scores(qi, j + 2, 2)
            ms = consume(j, 0, ms, c0)
            c3 = scores(qi, j + 3, 3)
            ms = consume(j + 1, 1, ms, c1)
            c0 = scores(qi, j + 4, 0)
            ms = consume(j + 2, 2, ms, c2)
            c1 = scores(qi, j + 5, 1)
            ms = consume(j + 3, 3, ms, c3)
            return ms, c0, c1

        ms, c0, c1 = lax.fori_loop(0, n_kv // 4 - 1, group, ((m0, m0),) + carry, unroll=True)
        q_next = jnp.minimum(qi + 1, nq - 1)
        c2 = scores(qi, n_kv - 2, 2)
        ms = consume(n_kv - 4, 0, ms, c0)
        c3 = scores(qi, n_kv - 1, 3)
        ms = consume(n_kv - 3, 1, ms, c1)
        c0 = scores(q_next, 0, 0)
        ms = consume(n_kv - 2, 2, ms, c2)
        c1 = scores(q_next, 1, 1)
        consume(n_kv - 1, 3, ms, c3)
        finish(qi)
        return c0, c1

    acc_ref[...] = jnp.zeros_like(acc_ref)
    lax.fori_loop(0, nq, q_tile, (scores(0, 0, 0), scores(0, 1, 1)), unroll=2)

    def emit(qi, carry):
        rows = pl.ds(pl.multiple_of(qi * tq, tq), tq)
        o_ref[rows, :] = ot_ref[qi].T.astype(BF16)
        return carry

    lax.fori_loop(0, nq, emit, 0)


def _attn_plain_body(qt_ref, k_ref, vt_ref, o_ref):
    nq, _, tq = qt_ref.shape
    n_kv = vt_ref.shape[0]
    ones = jnp.ones((SUM_ROWS, PLAIN_KEYS), BF16)

    def q_tile(qi, carry):
        n_steps = n_kv * KV_CHUNK // PLAIN_KEYS

        def scores(step, hh):
            rows = slice(step * PLAIN_KEYS, (step + 1) * PLAIN_KEYS)
            return _dot(k_ref[rows, hh * HEAD_PAD:(hh + 1) * HEAD_PAD],
                        qt_ref[qi, hh * HEAD_PAD:(hh + 1) * HEAD_PAD, :])

        acc = [None, None]
        st = [scores(0, 0), scores(0, 1)]
        for step in range(n_steps):
            st_next = [None, None]
            j, part = divmod(step * PLAIN_KEYS, KV_CHUNK)
            for hh in range(2):
                if step + 1 < n_steps:
                    st_next[hh] = scores(step + 1, hh)
                vt = jnp.concatenate(
                    [vt_ref[j, hh * V_HEAD:(hh + 1) * V_HEAD, part:part + PLAIN_KEYS], ones], axis=0)
                pv = _dot(vt, jnp.exp2(st[hh]).astype(BF16))
                acc[hh] = pv if acc[hh] is None else acc[hh] + pv
            st = st_next
        out = [a[:V_HEAD] / a[V_HEAD:V_HEAD + 1] for a in acc]
        rows = pl.ds(pl.multiple_of(qi * tq, tq), tq)
        o_ref[rows, :] = jnp.concatenate(out, axis=0).T.astype(BF16)
        return carry

    lax.fori_loop(0, nq, q_tile, 0, unroll=2)


def _attn(qt, k, vt, batch, seq, plain_ok):
    n = k.shape[0]
    nq = seq // Q_TILE
    n_kv = seq // KV_CHUNK
    acc = pltpu.VMEM((2, V_HEAD + SUM_ROWS, Q_TILE), F32)

    def call(body, scratch, name):
        return pl.pallas_call(
            body,
            grid=(batch, MLA_HEADS // 2),
            in_specs=[
                pl.BlockSpec((nq, 2 * HEAD_PAD, Q_TILE), lambda b, hp: (b, hp, 0)),
                pl.BlockSpec((seq, 2 * HEAD_PAD), lambda b, hp: (b, hp)),
                pl.BlockSpec((n_kv, 2 * V_HEAD, KV_CHUNK), lambda b, hp: (b, hp, 0)),
            ],
            out_specs=pl.BlockSpec((seq, 2 * V_HEAD), lambda b, hp: (b, hp)),
            out_shape=jax.ShapeDtypeStruct((n, MLA_HEADS * V_HEAD), BF16),
            scratch_shapes=scratch,
            compiler_params=_params(2),
            name=name,
        )

    flash = call(_attn_flash_body,
                 [pltpu.VMEM((2, 4, KV_CHUNK, Q_TILE), F32), acc,
                  pltpu.VMEM((nq, 2 * V_HEAD, Q_TILE), F32)], "attn_flash")
    plain = call(_attn_plain_body, [], "attn_plain")
    return lax.cond(plain_ok, plain, flash, qt, k, vt)


def _memkv_body(mem_ref, g_ref, w_ref, gk_ref, k_ref, v_ref):
    m = _rms(mem_ref[...], g_ref[...]).astype(BF16)
    kv = _dot(m, w_ref[...])
    width = XA_HEADS * XA_HEAD
    for hd in range(XA_HEADS):
        sl = slice(hd * XA_HEAD, (hd + 1) * XA_HEAD)
        k_ref[:, sl] = _rms(kv[:, sl], gk_ref[...]).astype(BF16)
    v_ref[...] = kv[:, width:].astype(BF16)


def _memkv(mem, w):
    n = mem.shape[0]
    width = XA_HEADS * XA_HEAD
    blk = lambda cols: pl.BlockSpec((N_MEM, cols), lambda b: (b, 0))
    consts = [w["g_mem"], w["w_mem_kv"], w["g_xk"]]
    return pl.pallas_call(
        _memkv_body,
        grid=(n // N_MEM,),
        in_specs=[blk(D_MODEL)] + [_const_spec(a.shape) for a in consts],
        out_specs=[blk(width), blk(width)],
        out_shape=[jax.ShapeDtypeStruct((n, width), BF16)] * 2,
        compiler_params=_params(1),
        name="memkv",
    )(mem, *consts)


def _mix_body(x_ref, xprev_ref, xnext_ref, o_ref, mk_ref, mv_ref, gmix_ref, win_ref,
              convw_ref, gxq_ref, womla_ref, woconv_ref, womem_ref,
              wout_ref, y_ref, u_ref, *, tiles_per_seq):
    i = pl.program_id(0)
    tm = x_ref.shape[0]
    x = x_ref[...]
    h = _rms(x, gmix_ref[...]).astype(BF16)

    o_cc, o_cx, o_xq = CONV_WIDTH, 2 * CONV_WIDTH, 3 * CONV_WIDTH
    o_gate = o_xq + XA_HEADS * XA_HEAD
    proj = lambda hh, start, width: _dot(hh, win_ref[:, start:start + width])

    def gated(hh):
        return proj(hh, o_cc, CONV_WIDTH) * proj(hh, o_cx, CONV_WIDTH)

    xq = proj(h, o_xq, XA_HEADS * XA_HEAD)
    first = (i % tiles_per_seq) == 0
    last = (i % tiles_per_seq) == tiles_per_seq - 1
    h_prev = _rms(xprev_ref[...], gmix_ref[...]).astype(BF16)
    h_next = _rms(xnext_ref[...], gmix_ref[...]).astype(BF16)
    u_ref[0:8, :] = jnp.where(first, 0.0, gated(h_prev))
    u_ref[8:8 + tm, :] = gated(h)
    u_ref[8 + tm:16 + tm, :] = jnp.where(last, 0.0, gated(h_next))
    cb = proj(h, 0, CONV_WIDTH)

    def mem_scores(hd):
        sl = slice(hd * XA_HEAD, (hd + 1) * XA_HEAD)
        qh = (_rms(xq[:, sl], gxq_ref[...]) * (XA_HEAD ** -0.5)).astype(BF16)
        return _dot_nt(qh, mk_ref[:, sl])

    def mem_head(hd, s):
        sl = slice(hd * XA_HEAD, (hd + 1) * XA_HEAD)
        p = jnp.exp(s - jnp.max(s, axis=-1, keepdims=True))
        l = jnp.sum(p, axis=-1, keepdims=True)
        return (_dot(p.astype(BF16), mv_ref[:, sl]) / l).astype(BF16)

    gate = lambda b: jax.nn.sigmoid(proj(h, o_gate + b * D_MODEL, D_MODEL))
    s_mem = [mem_scores(hd) for hd in range(XA_HEADS)]
    merged = gate(0) * _dot(o_ref[...], womla_ref[...])
    heads = [mem_head(0, s_mem[0])]
    cw = convw_ref[...]
    conv = (u_ref[7:7 + tm, :] * cw[0:1, :] + u_ref[8:8 + tm, :] * cw[1:2, :]
            + u_ref[9:9 + tm, :] * cw[2:3, :])
    merged = merged + gate(1) * _dot((cb * conv).astype(BF16), woconv_ref[...])
    heads += [mem_head(1, s_mem[1]), mem_head(2, s_mem[2])]
    gate_mem = gate(2)
    heads.append(mem_head(3, s_mem[3]))
    merged = merged + gate_mem * _dot(jnp.concatenate(heads, axis=-1), womem_ref[...])
    y_ref[...] = x + _dot(merged.astype(BF16), wout_ref[...])


def _mix(x1, o, mk, mv, seq, w, tm):
    n = x1.shape[0]
    tiles_per_seq = seq // tm
    halo = tm // 8
    n_halo = n // 8
    tok = lambda width: pl.BlockSpec((tm, width), lambda i: (i, 0))
    mem = pl.BlockSpec((N_MEM, XA_HEADS * XA_HEAD), lambda i: (i // tiles_per_seq, 0))
    consts = [w["g_mix"], w["w_in_mix"], w["conv_w"],
              w["g_xq"], w["w_o_mla"], w["w_o_conv"], w["w_o_mem"], w["w_out"]]
    return pl.pallas_call(
        functools.partial(_mix_body, tiles_per_seq=tiles_per_seq),
        grid=(n // tm,),
        in_specs=[
            tok(D_MODEL),
            pl.BlockSpec((8, D_MODEL), lambda i: (jnp.maximum(i * halo - 1, 0), 0)),
            pl.BlockSpec((8, D_MODEL), lambda i: (jnp.minimum((i + 1) * halo, n_halo - 1), 0)),
            tok(MLA_HEADS * V_HEAD), mem, mem,
        ] + [_const_spec(a.shape) for a in consts],
        out_specs=tok(D_MODEL),
        out_shape=jax.ShapeDtypeStruct((n, D_MODEL), F32),
        scratch_shapes=[pltpu.VMEM((tm + 16, CONV_WIDTH), F32)],
        compiler_params=_params(1),
        name="mix",
    )(x1, x1, x1, o, mk, mv, *consts)


def _pad_heads(w, heads, width):
    k = w.shape[0]
    w = w.reshape(k, heads, width)
    return jnp.pad(w, ((0, 0), (0, 0), (0, HEAD_PAD - width))).reshape(k, heads * HEAD_PAD)


def _ffn_weights(norm, w_gu, w_down):
    return norm.reshape(1, D_MODEL), w_gu.astype(BF16), w_down.astype(BF16)


def _logit_bound(gain_q, gain_k):
    return 1.02 * (QK_HEAD ** 0.5) * LOG2_E * jnp.max(jnp.abs(gain_q)) * jnp.max(jnp.abs(gain_k))


def _rope_tables(seq):
    half = QK_ROPE // 2
    inv_freq = ROPE_BASE ** (-jnp.arange(half, dtype=F32) / half)
    ang = jnp.arange(seq, dtype=jnp.int32).astype(F32)[:, None] * inv_freq[None, :]
    cos, sin = jnp.cos(ang), jnp.sin(ang)
    zeros = lambda width: jnp.zeros((seq, width), F32)
    tail = HEAD_PAD - QK_HEAD
    cos_t = jnp.concatenate([jnp.ones((seq, QK_NOPE), F32), cos, cos, zeros(tail)], axis=1)
    sin_t = jnp.concatenate([zeros(QK_NOPE), -sin, sin, zeros(tail)], axis=1)
    return cos_t, sin_t, cos.T, sin.T


def _trunk(x, mem, w, ffn1, ffn2, tables):
    batch, seq, _ = x.shape
    x = x.reshape(batch * seq, D_MODEL)
    x1 = _ffn(x, *ffn1, tm=512)
    qt, k, vt = _qkv(x1, seq, w, tables, tm=512)
    o = _attn(qt, k, vt, batch, seq, w["plain_ok"])
    mk, mv = _memkv(mem.reshape(batch * N_MEM, D_MODEL), w)
    x2 = _mix(x1, o, mk, mv, seq, w, tm=512)
    y = _ffn(x2, *ffn2, tm=512)
    return y.reshape(batch, seq, D_MODEL)


def kernel(x_prompt, x_sample, mem_prompt, mem_sample, ffn1_norm, ffn1_w_gu, ffn1_w_down, mix_norm, w_in, q_lora_norm, w_uq, kv_lora_norm, w_uk, w_uv, mla_q_norm, mla_k_norm, w_o_mla, conv_w, w_o_conv, mem_norm, w_mem_kv, xa_q_norm, xa_k_norm, w_o_mem, w_out, ffn2_norm, ffn2_w_gu, ffn2_w_down):
    ffn1 = _ffn_weights(ffn1_norm[0], ffn1_w_gu[0], ffn1_w_down[0])
    ffn2 = _ffn_weights(ffn2_norm[0], ffn2_w_gu[0], ffn2_w_down[0])

    w_in0 = w_in[0]
    o_cq, o_ckv, o_kr = 0, Q_LORA, Q_LORA + KV_LORA
    o_cb = o_kr + QK_ROPE
    cols = lambda start, width: w_in0[:, start:start + width]
    half = QK_ROPE // 2
    swap_halves = lambda a: jnp.concatenate([a[..., half:], a[..., :half]], axis=-1)
    rope_lanes = lambda a: jnp.pad(a, ((0, 0), (QK_NOPE, HEAD_PAD - QK_HEAD)))
    kr_pad = rope_lanes(cols(o_kr, QK_ROPE))
    kr_swap_pad = rope_lanes(swap_halves(cols(o_kr, QK_ROPE)))
    gk_swap = rope_lanes(swap_halves(mla_k_norm[0][QK_NOPE:]).reshape(1, QK_ROPE))
    row = lambda g: g.reshape(1, -1)
    pad_gain = lambda g: jnp.pad(g, (0, HEAD_PAD - QK_HEAD))
    w = {
        "g_mix": row(mix_norm[0]),
        "w_a": jnp.concatenate([cols(o_cq, Q_LORA), cols(o_ckv, KV_LORA), kr_pad, kr_swap_pad],
                               axis=1).astype(BF16),
        "g_q": row(q_lora_norm[0]),
        "w_uq_t": _pad_heads(w_uq[0], MLA_HEADS, QK_HEAD).T.astype(BF16),
        "g_kv": row(kv_lora_norm[0]),
        "w_uk": _pad_heads(w_uk[0], MLA_HEADS, QK_NOPE).astype(BF16),
        "w_uv_t": w_uv[0].T.astype(BF16),
        "g_qh_col": pad_gain(mla_q_norm[0]).reshape(HEAD_PAD, 1),
        "g_kh": pad_gain(mla_k_norm[0]).reshape(1, HEAD_PAD),
        "g_kh_swap": gk_swap,
        "w_in_mix": w_in0[:, o_cb:].astype(BF16),
        "conv_w": conv_w[0],
        "g_xq": row(xa_q_norm[0]),
        "w_o_mla": w_o_mla[0].astype(BF16),
        "w_o_conv": w_o_conv[0].astype(BF16),
        "w_o_mem": w_o_mem[0].astype(BF16),
        "w_out": w_out[0].astype(BF16),
        "g_mem": row(mem_norm[0]),
        "w_mem_kv": w_mem_kv[0].astype(BF16),
        "g_xk": row(xa_k_norm[0]),
        "plain_ok": _logit_bound(mla_q_norm[0], mla_k_norm[0]) <= MAX_PLAIN_LOGIT,
    }
    tables = _rope_tables(max(x_prompt.shape[1], x_sample.shape[1]))
    y_prompt = _trunk(x_prompt, mem_prompt, w, ffn1, ffn2, tables)
    y_sample = _trunk(x_sample, mem_sample, w, ffn1, ffn2, tables)
    return (y_prompt, y_sample)
```

```python
import functools

import jax
import jax.numpy as jnp
from jax import lax
from jax.experimental import pallas as pl
from jax.experimental.pallas import tpu as pltpu

D_MODEL = 1024
N_MEM = 256
MLA_HEADS = 8
QK_NOPE = 64
QK_ROPE = 32
QK_HEAD = QK_NOPE + QK_ROPE
V_HEAD = 64
Q_LORA = 384
KV_LORA = 256
CONV_WIDTH = 512
XA_HEADS = 4
XA_HEAD = 128
D_FF = 2816
ROPE_BASE = 10000.0
EPS = 1e-6
LOG2_E = 1.4426950408889634

LANES = 128
HEAD_PAD = LANES
FF_CHUNK = 256
N_FF_CHUNKS = D_FF // FF_CHUNK
KV_CHUNK = 512
Q_TILE = 512
SUM_ROWS = 16
HALO = 16
PLAIN_KEYS = 256
MAX_PLAIN_LOGIT = 60.0
VMEM_LIMIT = 56 * 1024 * 1024

BF16 = jnp.bfloat16
F32 = jnp.float32


def _const_spec(shape):
    zeros = (0,) * len(shape)
    return pl.BlockSpec(shape, lambda *_: zeros, pipeline_mode=pl.Buffered(1))


def _params(n_axes):
    return pltpu.CompilerParams(
        dimension_semantics=("arbitrary",) * n_axes,
        vmem_limit_bytes=VMEM_LIMIT)


def _rms(x, gain, n=None):
    n = x.shape[-1] if n is None else n
    inv = lax.rsqrt(jnp.sum(x * x, axis=-1, keepdims=True) * (1.0 / n) + EPS)
    return (x * inv) * gain


def _dot(a, b):
    return jnp.dot(a, b, preferred_element_type=F32)


def _dot_nt(a, b):
    return lax.dot_general(a, b, (((1,), (1,)), ((), ())), preferred_element_type=F32)


def _ffn_body(x_ref, g_ref, wgu_ref, wd_ref, o_ref, xn_ref, acc_ref):
    xn_ref[...] = _rms(x_ref[...], g_ref[...]).astype(BF16)
    acc_ref[...] = jnp.zeros_like(acc_ref)
    for c in range(N_FF_CHUNKS):
        cols = slice(c * FF_CHUNK, (c + 1) * FF_CHUNK)
        up_cols = slice(D_FF + c * FF_CHUNK, D_FF + (c + 1) * FF_CHUNK)
        xn = xn_ref[...]
        g = _dot(xn, wgu_ref[:, cols])
        u = _dot(xn, wgu_ref[:, up_cols])
        a = (g * jax.nn.sigmoid(g) * u).astype(BF16)
        acc_ref[...] += _dot(a, wd_ref[cols, :])
    o_ref[...] = x_ref[...] + 0.5 * acc_ref[...]


def _ffn(x, gain, wgu, wd, tm):
    n = x.shape[0]
    return pl.pallas_call(
        _ffn_body,
        grid=(n // tm,),
        in_specs=[
            pl.BlockSpec((tm, D_MODEL), lambda i: (i, 0)),
            _const_spec((1, D_MODEL)),
            _const_spec(wgu.shape),
            _const_spec(wd.shape),
        ],
        out_specs=pl.BlockSpec((tm, D_MODEL), lambda i: (i, 0)),
        out_shape=jax.ShapeDtypeStruct((n, D_MODEL), F32),
        scratch_shapes=[pltpu.VMEM((tm, D_MODEL), BF16), pltpu.VMEM((tm, D_MODEL), F32)],
        compiler_params=_params(1),
        name="ffn",
    )(x, gain, wgu, wd)


def _qkv_body(x_ref, gmix_ref, wa_ref, gq_ref, wuqt_ref, gkv_ref, wuk_ref, wuvt_ref,
              gqh_ref, gkh_ref, gkhs_ref, cos_ref, sin_ref, cost_ref, sint_ref,
              qt_ref, k_ref, vt_ref):
    tm = x_ref.shape[0]
    half = QK_ROPE // 2
    h = _rms(x_ref[...], gmix_ref[...]).astype(BF16)
    c = _dot(h, wa_ref[...])
    cq = _rms(c[:, :Q_LORA], gq_ref[...]).astype(BF16)
    ckv = _rms(c[:, Q_LORA:Q_LORA + KV_LORA], gkv_ref[...]).astype(BF16)
    o_kr = Q_LORA + KV_LORA
    kr = c[:, o_kr:o_kr + HEAD_PAD]
    kr_swap = c[:, o_kr + HEAD_PAD:]

    vt = _dot_nt(wuvt_ref[...], ckv).astype(BF16)
    for ch in range(tm // KV_CHUNK):
        vt_ref[ch] = vt[:, ch * KV_CHUNK:(ch + 1) * KV_CHUNK]

    kn = _dot(ckv, wuk_ref[...])
    cos, sin = cos_ref[...], sin_ref[...]
    for hd in range(MLA_HEADS):
        sl = slice(hd * HEAD_PAD, (hd + 1) * HEAD_PAD)
        kh = kn[:, sl] + kr
        inv = lax.rsqrt(jnp.sum(kh * kh, axis=-1, keepdims=True) * (1.0 / QK_HEAD) + EPS)
        rot = ((kh * inv) * gkh_ref[...]) * cos + ((kr_swap * inv) * gkhs_ref[...]) * sin
        k_ref[:, sl] = rot.astype(BF16)

    qt = _dot_nt(wuqt_ref[...], cq)
    cost, sint = cost_ref[...], sint_ref[...]
    scale = QK_HEAD ** -0.5 * LOG2_E

    def put(row, rows_f32):
        for t in range(tm // Q_TILE):
            qt_ref[t, row:row + rows_f32.shape[0], :] = (
                rows_f32[:, t * Q_TILE:(t + 1) * Q_TILE].astype(BF16))

    for hd in range(MLA_HEADS):
        r0 = hd * HEAD_PAD
        blk = qt[r0:r0 + HEAD_PAD, :]
        inv = lax.rsqrt(jnp.sum(blk * blk, axis=0, keepdims=True) * (1.0 / QK_HEAD) + EPS)
        qn = (blk * inv) * gqh_ref[...]
        x1, x2 = qn[QK_NOPE:QK_NOPE + half], qn[QK_NOPE + half:QK_HEAD]
        put(r0, qn[:QK_NOPE] * scale)
        put(r0 + QK_NOPE, (x1 * cost - x2 * sint) * scale)
        put(r0 + QK_NOPE + half, (x2 * cost + x1 * sint) * scale)
        put(r0 + QK_HEAD, jnp.zeros((HEAD_PAD - QK_HEAD, tm), F32))


def _qkv(x1, seq, w, tables, tm):
    n = x1.shape[0]
    tiles_per_seq = seq // tm
    tok = lambda width: pl.BlockSpec((tm, width), lambda i: (i, 0))
    tab = pl.BlockSpec((tm, HEAD_PAD), lambda i: (i % tiles_per_seq, 0))
    tab_t = pl.BlockSpec((QK_ROPE // 2, tm), lambda i: (0, i % tiles_per_seq))
    consts = [w["g_mix"], w["w_a"], w["g_q"], w["w_uq_t"], w["g_kv"], w["w_uk"], w["w_uv_t"],
              w["g_qh_col"], w["g_kh"], w["g_kh_swap"]]
    return pl.pallas_call(
        _qkv_body,
        grid=(n // tm,),
        in_specs=([tok(D_MODEL)] + [_const_spec(a.shape) for a in consts]
                  + [tab, tab, tab_t, tab_t]),
        out_specs=[
            pl.BlockSpec((tm // Q_TILE, MLA_HEADS * HEAD_PAD, Q_TILE), lambda i: (i, 0, 0)),
            tok(MLA_HEADS * HEAD_PAD),
            pl.BlockSpec((tm // KV_CHUNK, MLA_HEADS * V_HEAD, KV_CHUNK), lambda i: (i, 0, 0)),
        ],
        out_shape=[jax.ShapeDtypeStruct((n // Q_TILE, MLA_HEADS * HEAD_PAD, Q_TILE), BF16),
                   jax.ShapeDtypeStruct((n, MLA_HEADS * HEAD_PAD), BF16),
                   jax.ShapeDtypeStruct((n // KV_CHUNK, MLA_HEADS * V_HEAD, KV_CHUNK), BF16)],
        compiler_params=_params(1),
        name="qkv",
    )(x1, *consts, *tables)


def _attn_flash_body(qt_ref, k_ref, vt_ref, o_ref, s_ref, acc_ref, ot_ref):
    nq, _, tq = qt_ref.shape
    n_kv = vt_ref.shape[0]
    ones = jnp.ones((SUM_ROWS, KV_CHUNK), BF16)
    m0 = jnp.full((1, tq), -jnp.inf, F32)
    heads = range(2)

    def scores(qi, j, slot):
        rows = pl.ds(pl.multiple_of(j * KV_CHUNK, KV_CHUNK), KV_CHUNK)
        cmax = []
        for hh in heads:
            st = _dot(k_ref[rows, hh * HEAD_PAD:(hh + 1) * HEAD_PAD],
                      qt_ref[qi, hh * HEAD_PAD:(hh + 1) * HEAD_PAD, :])
            s_ref[hh, slot] = st
            cmax.append(jnp.max(st, axis=0, keepdims=True))
        return tuple(cmax)

    def consume(j, slot, ms, cmax):
        out = []
        for hh in heads:
            m_new = jnp.maximum(ms[hh], cmax[hh])
            alpha = jnp.exp2(ms[hh] - m_new)
            pt = jnp.exp2(s_ref[hh, slot] - m_new).astype(BF16)
            vt = jnp.concatenate([vt_ref[j, hh * V_HEAD:(hh + 1) * V_HEAD, :], ones], axis=0)
            acc_ref[hh] = alpha * acc_ref[hh] + _dot(vt, pt)
            out.append(m_new)
        return tuple(out)

    def finish(qi):
        for hh in heads:
            ot_ref[qi, hh * V_HEAD:(hh + 1) * V_HEAD, :] = (
                acc_ref[hh, :V_HEAD, :] / acc_ref[hh, V_HEAD:V_HEAD + 1, :])
        acc_ref[...] = jnp.zeros_like(acc_ref)

    def q_tile(qi, carry):
        def group(g, carry):
            ms, c0, c1 = carry
            j = 4 * g
            c2 = scores(qi, j + 2, 2)
            ms = consume(j, 0, ms, c0)
            c3 = scores(qi, j + 3, 3)
            ms = consume(j + 1, 1, ms, c1)
            c0 = scores(qi, j + 4, 0)
            ms = consume(j + 2, 2, ms, c2)
            c1 = scores(qi, j + 5, 1)
            ms = consume(j + 3, 3, ms, c3)
            return ms, c0, c1

        ms, c0, c1 = lax.fori_loop(0, n_kv // 4 - 1, group, ((m0, m0),) + carry, unroll=True)
        q_next = jnp.minimum(qi + 1, nq - 1)
        c2 = scores(qi, n_kv - 2, 2)
        ms = consume(n_kv - 4, 0, ms, c0)
        c3 = scores(qi, n_kv - 1, 3)
        ms = consume(n_kv - 3, 1, ms, c1)
        c0 = scores(q_next, 0, 0)
        ms = consume(n_kv - 2, 2, ms, c2)
        c1 = scores(q_next, 1, 1)
        consume(n_kv - 1, 3, ms, c3)
        finish(qi)
        return c0, c1

    acc_ref[...] = jnp.zeros_like(acc_ref)
    lax.fori_loop(0, nq, q_tile, (scores(0, 0, 0), scores(0, 1, 1)), unroll=2)

    def emit(qi, carry):
        rows = pl.ds(pl.multiple_of(qi * tq, tq), tq)
        o_ref[rows, :] = ot_ref[qi].T.astype(BF16)
        return carry

    lax.fori_loop(0, nq, emit, 0)


def _attn_plain_body(qt_ref, k_ref, vt_ref, o_ref):
    nq, _, tq = qt_ref.shape
    n_kv = vt_ref.shape[0]
    ones = jnp.ones((SUM_ROWS, PLAIN_KEYS), BF16)

    def q_tile(qi, carry):
        n_steps = n_kv * KV_CHUNK // PLAIN_KEYS

        def scores(step, hh):
            rows = slice(step * PLAIN_KEYS, (step + 1) * PLAIN_KEYS)
            return _dot(k_ref[rows, hh * HEAD_PAD:(hh + 1) * HEAD_PAD],
                        qt_ref[qi, hh * HEAD_PAD:(hh + 1) * HEAD_PAD, :])

        acc = [None, None]
        st = [scores(0, 0), scores(0, 1)]
        for step in range(n_steps):
            st_next = [None, None]
            j, part = divmod(step * PLAIN_KEYS, KV_CHUNK)
            for hh in range(2):
                if step + 1 < n_steps:
                    st_next[hh] = scores(step + 1, hh)
                vt = jnp.concatenate(
                    [vt_ref[j, hh * V_HEAD:(hh + 1) * V_HEAD, part:part + PLAIN_KEYS], ones], axis=0)
                pv = _dot(vt, jnp.exp2(st[hh]).astype(BF16))
                acc[hh] = pv if acc[hh] is None else acc[hh] + pv
            st = st_next
        out = [a[:V_HEAD] / a[V_HEAD:V_HEAD + 1] for a in acc]
        rows = pl.ds(pl.multiple_of(qi * tq, tq), tq)
        o_ref[rows, :] = jnp.concatenate(out, axis=0).T.astype(BF16)
        return carry

    lax.fori_loop(0, nq, q_tile, 0, unroll=2)


def _attn(qt, k, vt, batch, seq, plain_ok):
    n = k.shape[0]
    nq = seq // Q_TILE
    n_kv = seq // KV_CHUNK
    acc = pltpu.VMEM((2, V_HEAD + SUM_ROWS, Q_TILE), F32)

    def call(body, scratch, name):
        return pl.pallas_call(
            body,
            grid=(batch, MLA_HEADS // 2),
            in_specs=[
                pl.BlockSpec((nq, 2 * HEAD_PAD, Q_TILE), lambda b, hp: (b, hp, 0)),
                pl.BlockSpec((seq, 2 * HEAD_PAD), lambda b, hp: (b, hp)),
                pl.BlockSpec((n_kv, 2 * V_HEAD, KV_CHUNK), lambda b, hp: (b, hp, 0)),
            ],
            out_specs=pl.BlockSpec((seq, 2 * V_HEAD), lambda b, hp: (b, hp)),
            out_shape=jax.ShapeDtypeStruct((n, MLA_HEADS * V_HEAD), BF16),
            scratch_shapes=scratch,
            compiler_params=_params(2),
            name=name,
        )

    flash = call(_attn_flash_body,
                 [pltpu.VMEM((2, 4, KV_CHUNK, Q_TILE), F32), acc,
                  pltpu.VMEM((nq, 2 * V_HEAD, Q_TILE), F32)], "attn_flash")
    plain = call(_attn_plain_body, [], "attn_plain")
    return lax.cond(plain_ok, plain, flash, qt, k, vt)


def _memkv_body(mem_ref, g_ref, w_ref, gk_ref, k_ref, v_ref):
    m = _rms(mem_ref[...], g_ref[...]).astype(BF16)
    kv = _dot(m, w_ref[...])
    width = XA_HEADS * XA_HEAD
    for hd in range(XA_HEADS):
        sl = slice(hd * XA_HEAD, (hd + 1) * XA_HEAD)
        k_ref[:, sl] = _rms(kv[:, sl], gk_ref[...]).astype(BF16)
    v_ref[...] = kv[:, width:].astype(BF16)


def _memkv(mem, w):
    n = mem.shape[0]
    width = XA_HEADS * XA_HEAD
    blk = lambda cols: pl.BlockSpec((N_MEM, cols), lambda b: (b, 0))
    consts = [w["g_mem"], w["w_mem_kv"], w["g_xk"]]
    return pl.pallas_call(
        _memkv_body,
        grid=(n // N_MEM,),
        in_specs=[blk(D_MODEL)] + [_const_spec(a.shape) for a in consts],
        out_specs=[blk(width), blk(width)],
        out_shape=[jax.ShapeDtypeStruct((n, width), BF16)] * 2,
        compiler_params=_params(1),
        name="memkv",
    )(mem, *consts)


def _mix_body(x_ref, xprev_ref, xnext_ref, o_ref, mk_ref, mv_ref, gmix_ref, win_ref,
              convw_ref, gxq_ref, womla_ref, woconv_ref, womem_ref,
              wout_ref, y_ref, u_ref, *, tiles_per_seq):
    i = pl.program_id(0)
    tm = x_ref.shape[0]
    x = x_ref[...]
    h = _rms(x, gmix_ref[...]).astype(BF16)

    o_cc, o_cx, o_xq = CONV_WIDTH, 2 * CONV_WIDTH, 3 * CONV_WIDTH
    o_gate = o_xq + XA_HEADS * XA_HEAD
    proj = lambda hh, start, width: _dot(hh, win_ref[:, start:start + width])

    def gated(hh):
        return proj(hh, o_cc, CONV_WIDTH) * proj(hh, o_cx, CONV_WIDTH)

    xq = proj(h, o_xq, XA_HEADS * XA_HEAD)
    first = (i % tiles_per_seq) == 0
    last = (i % tiles_per_seq) == tiles_per_seq - 1
    h_prev = _rms(xprev_ref[...], gmix_ref[...]).astype(BF16)
    h_next = _rms(xnext_ref[...], gmix_ref[...]).astype(BF16)
    u = gated(jnp.concatenate([h_prev, h, h_next], axis=0))
    u_ref[0:HALO, :] = jnp.where(first, 0.0, u[:HALO])
    u_ref[HALO:HALO + tm, :] = u[HALO:HALO + tm]
    u_ref[HALO + tm:, :] = jnp.where(last, 0.0, u[HALO + tm:])
    cb = proj(h, 0, CONV_WIDTH)

    def mem_scores(hd):
        sl = slice(hd * XA_HEAD, (hd + 1) * XA_HEAD)
        qh = (_rms(xq[:, sl], gxq_ref[...]) * (XA_HEAD ** -0.5)).astype(BF16)
        return _dot_nt(qh, mk_ref[:, sl])

    def mem_head(hd, s):
        sl = slice(hd * XA_HEAD, (hd + 1) * XA_HEAD)
        p = jnp.exp(s - jnp.max(s, axis=-1, keepdims=True))
        l = jnp.sum(p, axis=-1, keepdims=True)
        return (_dot(p.astype(BF16), mv_ref[:, sl]) / l).astype(BF16)

    gate = lambda b: jax.nn.sigmoid(proj(h, o_gate + b * D_MODEL, D_MODEL))
    s_mem = [mem_scores(hd) for hd in range(XA_HEADS)]
    merged = gate(0) * _dot(o_ref[...], womla_ref[...])
    heads = [mem_head(0, s_mem[0])]
    cw = convw_ref[...]
    conv = (u_ref[HALO - 1:HALO - 1 + tm, :] * cw[0:1, :] + u_ref[HALO:HALO + tm, :] * cw[1:2, :]
            + u_ref[HALO + 1:HALO + 1 + tm, :] * cw[2:3, :])
    merged = merged + gate(1) * _dot((cb * conv).astype(BF16), woconv_ref[...])
    heads += [mem_head(1, s_mem[1]), mem_head(2, s_mem[2])]
    gate_mem = gate(2)
    heads.append(mem_head(3, s_mem[3]))
    merged = merged + gate_mem * _dot(jnp.concatenate(heads, axis=-1), womem_ref[...])
    y_ref[...] = x + _dot(merged.astype(BF16), wout_ref[...])


def _mix(x1, o, mk, mv, seq, w, tm):
    n = x1.shape[0]
    tiles_per_seq = seq // tm
    halo = tm // HALO
    n_halo = n // HALO
    tok = lambda width: pl.BlockSpec((tm, width), lambda i: (i, 0))
    mem = pl.BlockSpec((N_MEM, XA_HEADS * XA_HEAD), lambda i: (i // tiles_per_seq, 0))
    consts = [w["g_mix"], w["w_in_mix"], w["conv_w"],
              w["g_xq"], w["w_o_mla"], w["w_o_conv"], w["w_o_mem"], w["w_out"]]
    return pl.pallas_call(
        functools.partial(_mix_body, tiles_per_seq=tiles_per_seq),
        grid=(n // tm,),
        in_specs=[
            tok(D_MODEL),
            pl.BlockSpec((HALO, D_MODEL), lambda i: (jnp.maximum(i * halo - 1, 0), 0)),
            pl.BlockSpec((HALO, D_MODEL), lambda i: (jnp.minimum((i + 1) * halo, n_halo - 1), 0)),
            tok(MLA_HEADS * V_HEAD), mem, mem,
        ] + [_const_spec(a.shape) for a in consts],
        out_specs=tok(D_MODEL),
        out_shape=jax.ShapeDtypeStruct((n, D_MODEL), F32),
        scratch_shapes=[pltpu.VMEM((tm + 2 * HALO, CONV_WIDTH), F32)],
        compiler_params=_params(1),
        name="mix",
    )(x1, x1, x1, o, mk, mv, *consts)


def _pad_heads(w, heads, width):
    k = w.shape[0]
    w = w.reshape(k, heads, width)
    return jnp.pad(w, ((0, 0), (0, 0), (0, HEAD_PAD - width))).reshape(k, heads * HEAD_PAD)


def _ffn_weights(norm, w_gu, w_down):
    return norm.reshape(1, D_MODEL), w_gu.astype(BF16), w_down.astype(BF16)


def _logit_bound(gain_q, gain_k):
    return 1.02 * (QK_HEAD ** 0.5) * LOG2_E * jnp.max(jnp.abs(gain_q)) * jnp.max(jnp.abs(gain_k))


def _rope_tables(seq):
    half = QK_ROPE // 2
    inv_freq = ROPE_BASE ** (-jnp.arange(half, dtype=F32) / half)
    ang = jnp.arange(seq, dtype=jnp.int32).astype(F32)[:, None] * inv_freq[None, :]
    cos, sin = jnp.cos(ang), jnp.sin(ang)
    zeros = lambda width: jnp.zeros((seq, width), F32)
    tail = HEAD_PAD - QK_HEAD
    cos_t = jnp.concatenate([jnp.ones((seq, QK_NOPE), F32), cos, cos, zeros(tail)], axis=1)
    sin_t = jnp.concatenate([zeros(QK_NOPE), -sin, sin, zeros(tail)], axis=1)
    return cos_t, sin_t, cos.T, sin.T


def _trunk(x, mem, w, ffn1, ffn2, tables):
    batch, seq, _ = x.shape
    x = x.reshape(batch * seq, D_MODEL)
    x1 = _ffn(x, *ffn1, tm=512)
    qt, k, vt = _qkv(x1, seq, w, tables, tm=512)
    o = _attn(qt, k, vt, batch, seq, w["plain_ok"])
    mk, mv = _memkv(mem.reshape(batch * N_MEM, D_MODEL), w)
    x2 = _mix(x1, o, mk, mv, seq, w, tm=512)
    y = _ffn(x2, *ffn2, tm=512)
    return y.reshape(batch, seq, D_MODEL)


def kernel(x_prompt, x_sample, mem_prompt, mem_sample, ffn1_norm, ffn1_w_gu, ffn1_w_down, mix_norm, w_in, q_lora_norm, w_uq, kv_lora_norm, w_uk, w_uv, mla_q_norm, mla_k_norm, w_o_mla, conv_w, w_o_conv, mem_norm, w_mem_kv, xa_q_norm, xa_k_norm, w_o_mem, w_out, ffn2_norm, ffn2_w_gu, ffn2_w_down):
    ffn1 = _ffn_weights(ffn1_norm[0], ffn1_w_gu[0], ffn1_w_down[0])
    ffn2 = _ffn_weights(ffn2_norm[0], ffn2_w_gu[0], ffn2_w_down[0])

    w_in0 = w_in[0]
    o_cq, o_ckv, o_kr = 0, Q_LORA, Q_LORA + KV_LORA
    o_cb = o_kr + QK_ROPE
    cols = lambda start, width: w_in0[:, start:start + width]
    half = QK_ROPE // 2
    swap_halves = lambda a: jnp.concatenate([a[..., half:], a[..., :half]], axis=-1)
    rope_lanes = lambda a: jnp.pad(a, ((0, 0), (QK_NOPE, HEAD_PAD - QK_HEAD)))
    kr_pad = rope_lanes(cols(o_kr, QK_ROPE))
    kr_swap_pad = rope_lanes(swap_halves(cols(o_kr, QK_ROPE)))
    gk_swap = rope_lanes(swap_halves(mla_k_norm[0][QK_NOPE:]).reshape(1, QK_ROPE))
    row = lambda g: g.reshape(1, -1)
    pad_gain = lambda g: jnp.pad(g, (0, HEAD_PAD - QK_HEAD))
    w = {
        "g_mix": row(mix_norm[0]),
        "w_a": jnp.concatenate([cols(o_cq, Q_LORA), cols(o_ckv, KV_LORA), kr_pad, kr_swap_pad],
                               axis=1).astype(BF16),
        "g_q": row(q_lora_norm[0]),
        "w_uq_t": _pad_heads(w_uq[0], MLA_HEADS, QK_HEAD).T.astype(BF16),
        "g_kv": row(kv_lora_norm[0]),
        "w_uk": _pad_heads(w_uk[0], MLA_HEADS, QK_NOPE).astype(BF16),
        "w_uv_t": w_uv[0].T.astype(BF16),
        "g_qh_col": pad_gain(mla_q_norm[0]).reshape(HEAD_PAD, 1),
        "g_kh": pad_gain(mla_k_norm[0]).reshape(1, HEAD_PAD),
        "g_kh_swap": gk_swap,
        "w_in_mix": w_in0[:, o_cb:].astype(BF16),
        "conv_w": conv_w[0],
        "g_xq": row(xa_q_norm[0]),
        "w_o_mla": w_o_mla[0].astype(BF16),
        "w_o_conv": w_o_conv[0].astype(BF16),
        "w_o_mem": w_o_mem[0].astype(BF16),
        "w_out": w_out[0].astype(BF16),
        "g_mem": row(mem_norm[0]),
        "w_mem_kv": w_mem_kv[0].astype(BF16),
        "g_xk": row(xa_k_norm[0]),
        "plain_ok": _logit_bound(mla_q_norm[0], mla_k_norm[0]) <= MAX_PLAIN_LOGIT,
    }
    tables = _rope_tables(max(x_prompt.shape[1], x_sample.shape[1]))
    y_prompt = _trunk(x_prompt, mem_prompt, w, ffn1, ffn2, tables)
    y_sample = _trunk(x_sample, mem_sample, w, ffn1, ffn2, tables)
    return (y_prompt, y_sample)
```

```python
import functools

import jax
import jax.numpy as jnp
from jax import lax
from jax.experimental import pallas as pl
from jax.experimental.pallas import tpu as pltpu

D_MODEL = 1024
N_MEM = 256
MLA_HEADS = 8
QK_NOPE = 64
QK_ROPE = 32
QK_HEAD = QK_NOPE + QK_ROPE
V_HEAD = 64
Q_LORA = 384
KV_LORA = 256
CONV_WIDTH = 512
XA_HEADS = 4
XA_HEAD = 128
D_FF = 2816
ROPE_BASE = 10000.0
EPS = 1e-6
LOG2_E = 1.4426950408889634

LANES = 128
HEAD_PAD = LANES
FF_CHUNK = 256
N_FF_CHUNKS = D_FF // FF_CHUNK
KV_CHUNK = 512
Q_TILE = 512
SUM_ROWS = 16
HALO = 16
PLAIN_KEYS = 256
MAX_PLAIN_LOGIT = 60.0
VMEM_LIMIT = 56 * 1024 * 1024

BF16 = jnp.bfloat16
F32 = jnp.float32


def _const_spec(shape):
    zeros = (0,) * len(shape)
    return pl.BlockSpec(shape, lambda *_: zeros, pipeline_mode=pl.Buffered(1))


def _params(n_axes):
    return pltpu.CompilerParams(
        dimension_semantics=("arbitrary",) * n_axes,
        vmem_limit_bytes=VMEM_LIMIT)


def _rms(x, gain, n=None):
    n = x.shape[-1] if n is None else n
    inv = lax.rsqrt(jnp.sum(x * x, axis=-1, keepdims=True) * (1.0 / n) + EPS)
    return (x * inv) * gain


def _dot(a, b):
    return jnp.dot(a, b, preferred_element_type=F32)


def _dot_nt(a, b):
    return lax.dot_general(a, b, (((1,), (1,)), ((), ())), preferred_element_type=F32)


def _ffn_body(x_ref, g_ref, wgu_ref, wd_ref, o_ref, xn_ref, acc_ref):
    xn_ref[...] = _rms(x_ref[...], g_ref[...]).astype(BF16)
    acc_ref[...] = jnp.zeros_like(acc_ref)
    for c in range(N_FF_CHUNKS):
        cols = slice(c * FF_CHUNK, (c + 1) * FF_CHUNK)
        up_cols = slice(D_FF + c * FF_CHUNK, D_FF + (c + 1) * FF_CHUNK)
        xn = xn_ref[...]
        g = _dot(xn, wgu_ref[:, cols])
        u = _dot(xn, wgu_ref[:, up_cols])
        a = (g * jax.nn.sigmoid(g) * u).astype(BF16)
        acc_ref[...] += _dot(a, wd_ref[cols, :])
    o_ref[...] = x_ref[...] + 0.5 * acc_ref[...]


def _ffn(x, gain, wgu, wd, tm):
    n = x.shape[0]
    return pl.pallas_call(
        _ffn_body,
        grid=(n // tm,),
        in_specs=[
            pl.BlockSpec((tm, D_MODEL), lambda i: (i, 0)),
            _const_spec((1, D_MODEL)),
            _const_spec(wgu.shape),
            _const_spec(wd.shape),
        ],
        out_specs=pl.BlockSpec((tm, D_MODEL), lambda i: (i, 0)),
        out_shape=jax.ShapeDtypeStruct((n, D_MODEL), F32),
        scratch_shapes=[pltpu.VMEM((tm, D_MODEL), BF16), pltpu.VMEM((tm, D_MODEL), F32)],
        compiler_params=_params(1),
        name="ffn",
    )(x, gain, wgu, wd)


def _qkv_body(x_ref, gmix_ref, wa_ref, gq_ref, wuqt_ref, gkv_ref, wuk_ref, wuvt_ref,
              gqh_ref, gkh_ref, gkhs_ref, cos_ref, sin_ref, cost_ref, sint_ref,
              qt_ref, k_ref, vt_ref):
    tm = x_ref.shape[0]
    half = QK_ROPE // 2
    h = _rms(x_ref[...], gmix_ref[...]).astype(BF16)
    c = _dot(h, wa_ref[...])
    cq = _rms(c[:, :Q_LORA], gq_ref[...]).astype(BF16)
    ckv = _rms(c[:, Q_LORA:Q_LORA + KV_LORA], gkv_ref[...]).astype(BF16)
    o_kr = Q_LORA + KV_LORA
    kr = c[:, o_kr:o_kr + HEAD_PAD]
    kr_swap = c[:, o_kr + HEAD_PAD:]

    vt = _dot_nt(wuvt_ref[...], ckv).astype(BF16)
    for ch in range(tm // KV_CHUNK):
        vt_ref[ch] = vt[:, ch * KV_CHUNK:(ch + 1) * KV_CHUNK]

    kn = _dot(ckv, wuk_ref[...])
    cos, sin = cos_ref[...], sin_ref[...]
    for hd in range(MLA_HEADS):
        sl = slice(hd * HEAD_PAD, (hd + 1) * HEAD_PAD)
        kh = kn[:, sl] + kr
        inv = lax.rsqrt(jnp.sum(kh * kh, axis=-1, keepdims=True) * (1.0 / QK_HEAD) + EPS)
        rot = ((kh * inv) * gkh_ref[...]) * cos + ((kr_swap * inv) * gkhs_ref[...]) * sin
        k_ref[:, sl] = rot.astype(BF16)

    qt = _dot_nt(wuqt_ref[...], cq)
    cost, sint = cost_ref[...], sint_ref[...]
    scale = QK_HEAD ** -0.5 * LOG2_E

    def put(row, rows_f32):
        for t in range(tm // Q_TILE):
            qt_ref[t, row:row + rows_f32.shape[0], :] = (
                rows_f32[:, t * Q_TILE:(t + 1) * Q_TILE].astype(BF16))

    for hd in range(MLA_HEADS):
        r0 = hd * HEAD_PAD
        blk = qt[r0:r0 + HEAD_PAD, :]
        inv = lax.rsqrt(jnp.sum(blk * blk, axis=0, keepdims=True) * (1.0 / QK_HEAD) + EPS)
        qn = (blk * inv) * gqh_ref[...]
        x1, x2 = qn[QK_NOPE:QK_NOPE + half], qn[QK_NOPE + half:QK_HEAD]
        put(r0, qn[:QK_NOPE] * scale)
        put(r0 + QK_NOPE, (x1 * cost - x2 * sint) * scale)
        put(r0 + QK_NOPE + half, (x2 * cost + x1 * sint) * scale)
        put(r0 + QK_HEAD, jnp.zeros((HEAD_PAD - QK_HEAD, tm), F32))


def _qkv(x1, seq, w, tables, tm):
    n = x1.shape[0]
    tiles_per_seq = seq // tm
    tok = lambda width: pl.BlockSpec((tm, width), lambda i: (i, 0))
    tab = pl.BlockSpec((tm, HEAD_PAD), lambda i: (i % tiles_per_seq, 0))
    tab_t = pl.BlockSpec((QK_ROPE // 2, tm), lambda i: (0, i % tiles_per_seq))
    consts = [w["g_mix"], w["w_a"], w["g_q"], w["w_uq_t"], w["g_kv"], w["w_uk"], w["w_uv_t"],
              w["g_qh_col"], w["g_kh"], w["g_kh_swap"]]
    return pl.pallas_call(
        _qkv_body,
        grid=(n // tm,),
        in_specs=([tok(D_MODEL)] + [_const_spec(a.shape) for a in consts]
                  + [tab, tab, tab_t, tab_t]),
        out_specs=[
            pl.BlockSpec((tm // Q_TILE, MLA_HEADS * HEAD_PAD, Q_TILE), lambda i: (i, 0, 0)),
            tok(MLA_HEADS * HEAD_PAD),
            pl.BlockSpec((tm // KV_CHUNK, MLA_HEADS * V_HEAD, KV_CHUNK), lambda i: (i, 0, 0)),
        ],
        out_shape=[jax.ShapeDtypeStruct((n // Q_TILE, MLA_HEADS * HEAD_PAD, Q_TILE), BF16),
                   jax.ShapeDtypeStruct((n, MLA_HEADS * HEAD_PAD), BF16),
                   jax.ShapeDtypeStruct((n // KV_CHUNK, MLA_HEADS * V_HEAD, KV_CHUNK), BF16)],
        compiler_params=_params(1),
        name="qkv",
    )(x1, *consts, *tables)


def _attn_flash_body(qt_ref, k_ref, vt_ref, o_ref, s_ref, acc_ref, ot_ref):
    nq, _, tq = qt_ref.shape
    n_kv = vt_ref.shape[0]
    ones = jnp.ones((SUM_ROWS, KV_CHUNK), BF16)
    m0 = jnp.full((1, tq), -jnp.inf, F32)
    heads = range(2)

    def scores(qi, j, slot):
        rows = pl.ds(pl.multiple_of(j * KV_CHUNK, KV_CHUNK), KV_CHUNK)
        cmax = []
        for hh in heads:
            st = _dot(k_ref[rows, hh * HEAD_PAD:(hh + 1) * HEAD_PAD],
                      qt_ref[qi, hh * HEAD_PAD:(hh + 1) * HEAD_PAD, :])
            s_ref[hh, slot] = st
            cmax.append(jnp.max(st, axis=0, keepdims=True))
        return tuple(cmax)

    def consume(j, slot, ms, cmax):
        out = []
        for hh in heads:
            m_new = jnp.maximum(ms[hh], cmax[hh])
            alpha = jnp.exp2(ms[hh] - m_new)
            pt = jnp.exp2(s_ref[hh, slot] - m_new).astype(BF16)
            vt = jnp.concatenate([vt_ref[j, hh * V_HEAD:(hh + 1) * V_HEAD, :], ones], axis=0)
            acc_ref[hh] = alpha * acc_ref[hh] + _dot(vt, pt)
            out.append(m_new)
        return tuple(out)

    def finish(qi):
        for hh in heads:
            ot_ref[qi, hh * V_HEAD:(hh + 1) * V_HEAD, :] = (
                acc_ref[hh, :V_HEAD, :] / acc_ref[hh, V_HEAD:V_HEAD + 1, :])
        acc_ref[...] = jnp.zeros_like(acc_ref)

    def q_tile(qi, carry):
        def group(g, carry):
            ms, c0, c1 = carry
            j = 4 * g
            c2 = scores(qi, j + 2, 2)
            ms = consume(j, 0, ms, c0)
            c3 = scores(qi, j + 3, 3)
            ms = consume(j + 1, 1, ms, c1)
            c0 = scores(qi, j + 4, 0)
            ms = consume(j + 2, 2, ms, c2)
            c1 = scores(qi, j + 5, 1)
            ms = consume(j + 3, 3, ms, c3)
            return ms, c0, c1

        ms, c0, c1 = lax.fori_loop(0, n_kv // 4 - 1, group, ((m0, m0),) + carry, unroll=True)
        q_next = jnp.minimum(qi + 1, nq - 1)
        c2 = scores(qi, n_kv - 2, 2)
        ms = consume(n_kv - 4, 0, ms, c0)
        c3 = scores(qi, n_kv - 1, 3)
        ms = consume(n_kv - 3, 1, ms, c1)
        c0 = scores(q_next, 0, 0)
        ms = consume(n_kv - 2, 2, ms, c2)
        c1 = scores(q_next, 1, 1)
        consume(n_kv - 1, 3, ms, c3)
        finish(qi)
        return c0, c1

    acc_ref[...] = jnp.zeros_like(acc_ref)
    lax.fori_loop(0, nq, q_tile, (scores(0, 0, 0), scores(0, 1, 1)), unroll=2)

    def emit(qi, carry):
        rows = pl.ds(pl.multiple_of(qi * tq, tq), tq)
        o_ref[rows, :] = ot_ref[qi].T.astype(BF16)
        return carry

    lax.fori_loop(0, nq, emit, 0)


def _attn_plain_body(qt_ref, k_ref, vt_ref, o_ref):
    nq, _, tq = qt_ref.shape
    n_kv = vt_ref.shape[0]
    ones = jnp.ones((SUM_ROWS, PLAIN_KEYS), BF16)

    def q_tile(qi, carry):
        n_steps = n_kv * KV_CHUNK // PLAIN_KEYS

        def scores(step, hh):
            rows = slice(step * PLAIN_KEYS, (step + 1) * PLAIN_KEYS)
            return _dot(k_ref[rows, hh * HEAD_PAD:(hh + 1) * HEAD_PAD],
                        qt_ref[qi, hh * HEAD_PAD:(hh + 1) * HEAD_PAD, :])

        acc = [None, None]
        st = [scores(0, 0), scores(0, 1)]
        for step in range(n_steps):
            st_next = [None, None]
            j, part = divmod(step * PLAIN_KEYS, KV_CHUNK)
            for hh in range(2):
                if step + 1 < n_steps:
                    st_next[hh] = scores(step + 1, hh)
                vt = jnp.concatenate(
                    [vt_ref[j, hh * V_HEAD:(hh + 1) * V_HEAD, part:part + PLAIN_KEYS], ones], axis=0)
                pv = _dot(vt, jnp.exp2(st[hh]).astype(BF16))
                acc[hh] = pv if acc[hh] is None else acc[hh] + pv
            st = st_next
        out = [a[:V_HEAD] / a[V_HEAD:V_HEAD + 1] for a in acc]
        rows = pl.ds(pl.multiple_of(qi * tq, tq), tq)
        o_ref[rows, :] = jnp.concatenate(out, axis=0).T.astype(BF16)
        return carry

    lax.fori_loop(0, nq, q_tile, 0, unroll=2)


def _attn(qt, k, vt, batch, seq, plain_ok):
    n = k.shape[0]
    nq = seq // Q_TILE
    n_kv = seq // KV_CHUNK
    acc = pltpu.VMEM((2, V_HEAD + SUM_ROWS, Q_TILE), F32)

    def call(body, scratch, name):
        return pl.pallas_call(
            body,
            grid=(batch, MLA_HEADS // 2),
            in_specs=[
                pl.BlockSpec((nq, 2 * HEAD_PAD, Q_TILE), lambda b, hp: (b, hp, 0)),
                pl.BlockSpec((seq, 2 * HEAD_PAD), lambda b, hp: (b, hp)),
                pl.BlockSpec((n_kv, 2 * V_HEAD, KV_CHUNK), lambda b, hp: (b, hp, 0)),
            ],
            out_specs=pl.BlockSpec((seq, 2 * V_HEAD), lambda b, hp: (b, hp)),
            out_shape=jax.ShapeDtypeStruct((n, MLA_HEADS * V_HEAD), BF16),
            scratch_shapes=scratch,
            compiler_params=_params(2),
            name=name,
        )

    flash = call(_attn_flash_body,
                 [pltpu.VMEM((2, 4, KV_CHUNK, Q_TILE), F32), acc,
                  pltpu.VMEM((nq, 2 * V_HEAD, Q_TILE), F32)], "attn_flash")
    plain = call(_attn_plain_body, [], "attn_plain")
    return lax.cond(plain_ok, plain, flash, qt, k, vt)


def _memkv_body(mem_ref, g_ref, w_ref, gk_ref, k_ref, v_ref):
    m = _rms(mem_ref[...], g_ref[...]).astype(BF16)
    kv = _dot(m, w_ref[...])
    width = XA_HEADS * XA_HEAD
    for hd in range(XA_HEADS):
        sl = slice(hd * XA_HEAD, (hd + 1) * XA_HEAD)
        k_ref[:, sl] = _rms(kv[:, sl], gk_ref[...]).astype(BF16)
    v_ref[...] = kv[:, width:].astype(BF16)


def _memkv(mem, w):
    n = mem.shape[0]
    width = XA_HEADS * XA_HEAD
    blk = lambda cols: pl.BlockSpec((N_MEM, cols), lambda b: (b, 0))
    consts = [w["g_mem"], w["w_mem_kv"], w["g_xk"]]
    return pl.pallas_call(
        _memkv_body,
        grid=(n // N_MEM,),
        in_specs=[blk(D_MODEL)] + [_const_spec(a.shape) for a in consts],
        out_specs=[blk(width), blk(width)],
        out_shape=[jax.ShapeDtypeStruct((n, width), BF16)] * 2,
        compiler_params=_params(1),
        name="memkv",
    )(mem, *consts)


def _mix_body(x_ref, xprev_ref, xnext_ref, o_ref, mk_ref, mv_ref, gmix_ref, win_ref,
              convw_ref, gxq_ref, womla_ref, woconv_ref, womem_ref,
              wout_ref, y_ref, u_ref, *, tiles_per_seq):
    i = pl.program_id(0)
    tm = x_ref.shape[0]
    x = x_ref[...]
    h = _rms(x, gmix_ref[...]).astype(BF16)

    o_cc, o_cx, o_xq = CONV_WIDTH, 2 * CONV_WIDTH, 3 * CONV_WIDTH
    o_gate = o_xq + XA_HEADS * XA_HEAD
    proj = lambda hh, start, width: _dot(hh, win_ref[:, start:start + width])

    def gated(hh):
        return proj(hh, o_cc, CONV_WIDTH) * proj(hh, o_cx, CONV_WIDTH)

    xq = proj(h, o_xq, XA_HEADS * XA_HEAD)
    first = (i % tiles_per_seq) == 0
    last = (i % tiles_per_seq) == tiles_per_seq - 1
    h_prev = _rms(xprev_ref[...], gmix_ref[...]).astype(BF16)
    h_next = _rms(xnext_ref[...], gmix_ref[...]).astype(BF16)
    u = gated(jnp.concatenate([h_prev, h, h_next], axis=0))
    u_ref[0:HALO, :] = jnp.where(first, 0.0, u[:HALO])
    u_ref[HALO:HALO + tm, :] = u[HALO:HALO + tm]
    u_ref[HALO + tm:, :] = jnp.where(last, 0.0, u[HALO + tm:])
    cb = proj(h, 0, CONV_WIDTH)

    def mem_scores(hd):
        sl = slice(hd * XA_HEAD, (hd + 1) * XA_HEAD)
        qh = (_rms(xq[:, sl], gxq_ref[...]) * (XA_HEAD ** -0.5)).astype(BF16)
        return _dot_nt(qh, mk_ref[:, sl])

    def mem_head(hd, s):
        sl = slice(hd * XA_HEAD, (hd + 1) * XA_HEAD)
        p = jnp.exp(s - jnp.max(s, axis=-1, keepdims=True))
        l = jnp.sum(p, axis=-1, keepdims=True)
        return (_dot(p.astype(BF16), mv_ref[:, sl]) / l).astype(BF16)

    gate = lambda b: jax.nn.sigmoid(proj(h, o_gate + b * D_MODEL, D_MODEL))
    s_mem = [mem_scores(hd) for hd in range(XA_HEADS)]
    merged = gate(0) * _dot(o_ref[...], womla_ref[...])
    heads = [mem_head(0, s_mem[0])]
    cw = convw_ref[...]
    conv = (u_ref[HALO - 1:HALO - 1 + tm, :] * cw[0:1, :] + u_ref[HALO:HALO + tm, :] * cw[1:2, :]
            + u_ref[HALO + 1:HALO + 1 + tm, :] * cw[2:3, :])
    merged = merged + gate(1) * _dot((cb * conv).astype(BF16), woconv_ref[...])
    heads += [mem_head(1, s_mem[1]), mem_head(2, s_mem[2])]
    gate_mem = gate(2)
    heads.append(mem_head(3, s_mem[3]))
    merged = merged + gate_mem * _dot(jnp.concatenate(heads, axis=-1), womem_ref[...])
    y_ref[...] = x + _dot(merged.astype(BF16), wout_ref[...])


def _mix(x1, o, mk, mv, seq, w, tm):
    n = x1.shape[0]
    tiles_per_seq = seq // tm
    halo = tm // HALO
    n_halo = n // HALO
    tok = lambda width: pl.BlockSpec((tm, width), lambda i: (i, 0))
    mem = pl.BlockSpec((N_MEM, XA_HEADS * XA_HEAD), lambda i: (i // tiles_per_seq, 0))
    consts = [w["g_mix"], w["w_in_mix"], w["conv_w"],
              w["g_xq"], w["w_o_mla"], w["w_o_conv"], w["w_o_mem"], w["w_out"]]
    return pl.pallas_call(
        functools.partial(_mix_body, tiles_per_seq=tiles_per_seq),
        grid=(n // tm,),
        in_specs=[
            tok(D_MODEL),
            pl.BlockSpec((HALO, D_MODEL), lambda i: (jnp.maximum(i * halo - 1, 0), 0)),
            pl.BlockSpec((HALO, D_MODEL), lambda i: (jnp.minimum((i + 1) * halo, n_halo - 1), 0)),
            tok(MLA_HEADS * V_HEAD), mem, mem,
        ] + [_const_spec(a.shape) for a in consts],
        out_specs=tok(D_MODEL),
        out_shape=jax.ShapeDtypeStruct((n, D_MODEL), F32),
        scratch_shapes=[pltpu.VMEM((tm + 2 * HALO, CONV_WIDTH), F32)],
        compiler_params=_params(1),
        name="mix",
    )(x1, x1, x1, o, mk, mv, *consts)


def _pad_heads(w, heads, width):
    k = w.shape[0]
    w = w.reshape(k, heads, width)
    return jnp.pad(w, ((0, 0), (0, 0), (0, HEAD_PAD - width))).reshape(k, heads * HEAD_PAD)


def _ffn_weights(norm, w_gu, w_down):
    return norm.reshape(1, D_MODEL), w_gu.astype(BF16), w_down.astype(BF16)


def _logit_bound(gain_q, gain_k):
    return 1.02 * (QK_HEAD ** 0.5) * LOG2_E * jnp.max(jnp.abs(gain_q)) * jnp.max(jnp.abs(gain_k))


def _rope_tables(seq):
    half = QK_ROPE // 2
    inv_freq = ROPE_BASE ** (-jnp.arange(half, dtype=F32) / half)
    ang = jnp.arange(seq, dtype=jnp.int32).astype(F32)[:, None] * inv_freq[None, :]
    cos, sin = jnp.cos(ang), jnp.sin(ang)
    zeros = lambda width: jnp.zeros((seq, width), F32)
    tail = HEAD_PAD - QK_HEAD
    cos_t = jnp.concatenate([jnp.ones((seq, QK_NOPE), F32), cos, cos, zeros(tail)], axis=1)
    sin_t = jnp.concatenate([zeros(QK_NOPE), -sin, sin, zeros(tail)], axis=1)
    return cos_t, sin_t, cos.T, sin.T


def _trunk(x, mem, w, ffn1, ffn2, tables):
    batch, seq, _ = x.shape
    x = x.reshape(batch * seq, D_MODEL)
    x1 = _ffn(x, *ffn1, tm=1024)
    qt, k, vt = _qkv(x1, seq, w, tables, tm=1024)
    o = _attn(qt, k, vt, batch, seq, w["plain_ok"])
    mk, mv = _memkv(mem.reshape(batch * N_MEM, D_MODEL), w)
    x2 = _mix(x1, o, mk, mv, seq, w, tm=512)
    y = _ffn(x2, *ffn2, tm=1024)
    return y.reshape(batch, seq, D_MODEL)


def kernel(x_prompt, x_sample, mem_prompt, mem_sample, ffn1_norm, ffn1_w_gu, ffn1_w_down, mix_norm, w_in, q_lora_norm, w_uq, kv_lora_norm, w_uk, w_uv, mla_q_norm, mla_k_norm, w_o_mla, conv_w, w_o_conv, mem_norm, w_mem_kv, xa_q_norm, xa_k_norm, w_o_mem, w_out, ffn2_norm, ffn2_w_gu, ffn2_w_down):
    ffn1 = _ffn_weights(ffn1_norm[0], ffn1_w_gu[0], ffn1_w_down[0])
    ffn2 = _ffn_weights(ffn2_norm[0], ffn2_w_gu[0], ffn2_w_down[0])

    w_in0 = w_in[0]
    o_cq, o_ckv, o_kr = 0, Q_LORA, Q_LORA + KV_LORA
    o_cb = o_kr + QK_ROPE
    cols = lambda start, width: w_in0[:, start:start + width]
    half = QK_ROPE // 2
    swap_halves = lambda a: jnp.concatenate([a[..., half:], a[..., :half]], axis=-1)
    rope_lanes = lambda a: jnp.pad(a, ((0, 0), (QK_NOPE, HEAD_PAD - QK_HEAD)))
    kr_pad = rope_lanes(cols(o_kr, QK_ROPE))
    kr_swap_pad = rope_lanes(swap_halves(cols(o_kr, QK_ROPE)))
    gk_swap = rope_lanes(swap_halves(mla_k_norm[0][QK_NOPE:]).reshape(1, QK_ROPE))
    row = lambda g: g.reshape(1, -1)
    pad_gain = lambda g: jnp.pad(g, (0, HEAD_PAD - QK_HEAD))
    w = {
        "g_mix": row(mix_norm[0]),
        "w_a": jnp.concatenate([cols(o_cq, Q_LORA), cols(o_ckv, KV_LORA), kr_pad, kr_swap_pad],
                               axis=1).astype(BF16),
        "g_q": row(q_lora_norm[0]),
        "w_uq_t": _pad_heads(w_uq[0], MLA_HEADS, QK_HEAD).T.astype(BF16),
        "g_kv": row(kv_lora_norm[0]),
        "w_uk": _pad_heads(w_uk[0], MLA_HEADS, QK_NOPE).astype(BF16),
        "w_uv_t": w_uv[0].T.astype(BF16),
        "g_qh_col": pad_gain(mla_q_norm[0]).reshape(HEAD_PAD, 1),
        "g_kh": pad_gain(mla_k_norm[0]).reshape(1, HEAD_PAD),
        "g_kh_swap": gk_swap,
        "w_in_mix": w_in0[:, o_cb:].astype(BF16),
        "conv_w": conv_w[0],
        "g_xq": row(xa_q_norm[0]),
        "w_o_mla": w_o_mla[0].astype(BF16),
        "w_o_conv": w_o_conv[0].astype(BF16),
        "w_o_mem": w_o_mem[0].astype(BF16),
        "w_out": w_out[0].astype(BF16),
        "g_mem": row(mem_norm[0]),
        "w_mem_kv": w_mem_kv[0].astype(BF16),
        "g_xk": row(xa_k_norm[0]),
        "plain_ok": _logit_bound(mla_q_norm[0], mla_k_norm[0]) <= MAX_PLAIN_LOGIT,
    }
    tables = _rope_tables(max(x_prompt.shape[1], x_sample.shape[1]))
    y_prompt = _trunk(x_prompt, mem_prompt, w, ffn1, ffn2, tables)
    y_sample = _trunk(x_sample, mem_sample, w, ffn1, ffn2, tables)
    return (y_prompt, y_sample)
```

```python
import functools

import jax
import jax.numpy as jnp
from jax import lax
from jax.experimental import pallas as pl
from jax.experimental.pallas import tpu as pltpu

D_MODEL = 1024
N_MEM = 256
MLA_HEADS = 8
QK_NOPE = 64
QK_ROPE = 32
QK_HEAD = QK_NOPE + QK_ROPE
V_HEAD = 64
Q_LORA = 384
KV_LORA = 256
CONV_WIDTH = 512
XA_HEADS = 4
XA_HEAD = 128
D_FF = 2816
ROPE_BASE = 10000.0
EPS = 1e-6
LOG2_E = 1.4426950408889634

LANES = 128
HEAD_PAD = LANES
FF_CHUNK = 256
N_FF_CHUNKS = D_FF // FF_CHUNK
KV_CHUNK = 512
Q_TILE = 512
SUM_ROWS = 16
HALO = 16
PLAIN_KEYS = 256
MAX_PLAIN_LOGIT = 60.0
VMEM_LIMIT = 56 * 1024 * 1024

BF16 = jnp.bfloat16
F32 = jnp.float32


def _const_spec(shape):
    zeros = (0,) * len(shape)
    return pl.BlockSpec(shape, lambda *_: zeros, pipeline_mode=pl.Buffered(1))


def _params(n_axes):
    return pltpu.CompilerParams(
        dimension_semantics=("arbitrary",) * n_axes,
        vmem_limit_bytes=VMEM_LIMIT)


def _rms(x, gain, n=None):
    n = x.shape[-1] if n is None else n
    inv = lax.rsqrt(jnp.sum(x * x, axis=-1, keepdims=True) * (1.0 / n) + EPS)
    return (x * inv) * gain


def _dot(a, b):
    return jnp.dot(a, b, preferred_element_type=F32)


def _dot_nt(a, b):
    return lax.dot_general(a, b, (((1,), (1,)), ((), ())), preferred_element_type=F32)


def _ffn_body(x_ref, g_ref, wgu_ref, wd_ref, o_ref, xn_ref, acc_ref):
    xn_ref[...] = _rms(x_ref[...], g_ref[...]).astype(BF16)
    acc_ref[...] = jnp.zeros_like(acc_ref)
    for c in range(N_FF_CHUNKS):
        cols = slice(c * FF_CHUNK, (c + 1) * FF_CHUNK)
        up_cols = slice(D_FF + c * FF_CHUNK, D_FF + (c + 1) * FF_CHUNK)
        xn = xn_ref[...]
        g = _dot(xn, wgu_ref[:, cols])
        u = _dot(xn, wgu_ref[:, up_cols])
        a = (g * jax.nn.sigmoid(g) * u).astype(BF16)
        acc_ref[...] += _dot(a, wd_ref[cols, :])
    o_ref[...] = x_ref[...] + 0.5 * acc_ref[...]


def _ffn(x, gain, wgu, wd, tm):
    n = x.shape[0]
    return pl.pallas_call(
        _ffn_body,
        grid=(n // tm,),
        in_specs=[
            pl.BlockSpec((tm, D_MODEL), lambda i: (i, 0)),
            _const_spec((1, D_MODEL)),
            _const_spec(wgu.shape),
            _const_spec(wd.shape),
        ],
        out_specs=pl.BlockSpec((tm, D_MODEL), lambda i: (i, 0)),
        out_shape=jax.ShapeDtypeStruct((n, D_MODEL), F32),
        scratch_shapes=[pltpu.VMEM((tm, D_MODEL), BF16), pltpu.VMEM((tm, D_MODEL), F32)],
        compiler_params=_params(1),
        name="ffn",
    )(x, gain, wgu, wd)


def _qkv_body(x_ref, gmix_ref, wa_ref, gq_ref, wuqt_ref, gkv_ref, wuk_ref, wuvt_ref,
              gqh_ref, gkh_ref, gkhs_ref, cos_ref, sin_ref, cost_ref, sint_ref,
              qt_ref, k_ref, vt_ref):
    tm = x_ref.shape[0]
    half = QK_ROPE // 2
    h = _rms(x_ref[...], gmix_ref[...]).astype(BF16)
    c = _dot(h, wa_ref[...])
    cq = _rms(c[:, :Q_LORA], gq_ref[...]).astype(BF16)
    ckv = _rms(c[:, Q_LORA:Q_LORA + KV_LORA], gkv_ref[...]).astype(BF16)
    o_kr = Q_LORA + KV_LORA
    kr = c[:, o_kr:o_kr + HEAD_PAD]
    kr_swap = c[:, o_kr + HEAD_PAD:]

    vt = _dot_nt(wuvt_ref[...], ckv).astype(BF16)
    for ch in range(tm // KV_CHUNK):
        vt_ref[ch] = vt[:, ch * KV_CHUNK:(ch + 1) * KV_CHUNK]

    kn = _dot(ckv, wuk_ref[...])
    cos, sin = cos_ref[...], sin_ref[...]
    for hd in range(MLA_HEADS):
        sl = slice(hd * HEAD_PAD, (hd + 1) * HEAD_PAD)
        kh = kn[:, sl] + kr
        inv = lax.rsqrt(jnp.sum(kh * kh, axis=-1, keepdims=True) * (1.0 / QK_HEAD) + EPS)
        rot = ((kh * inv) * gkh_ref[...]) * cos + ((kr_swap * inv) * gkhs_ref[...]) * sin
        k_ref[:, sl] = rot.astype(BF16)

    qt = _dot_nt(wuqt_ref[...], cq)
    cost, sint = cost_ref[...], sint_ref[...]
    scale = QK_HEAD ** -0.5 * LOG2_E

    def put(row, rows_f32):
        for t in range(tm // Q_TILE):
            qt_ref[t, row:row + rows_f32.shape[0], :] = (
                rows_f32[:, t * Q_TILE:(t + 1) * Q_TILE].astype(BF16))

    for hd in range(MLA_HEADS):
        r0 = hd * HEAD_PAD
        blk = qt[r0:r0 + HEAD_PAD, :]
        inv = lax.rsqrt(jnp.sum(blk * blk, axis=0, keepdims=True) * (1.0 / QK_HEAD) + EPS)
        qn = (blk * inv) * gqh_ref[...]
        x1, x2 = qn[QK_NOPE:QK_NOPE + half], qn[QK_NOPE + half:QK_HEAD]
        put(r0, qn[:QK_NOPE] * scale)
        put(r0 + QK_NOPE, (x1 * cost - x2 * sint) * scale)
        put(r0 + QK_NOPE + half, (x2 * cost + x1 * sint) * scale)
        put(r0 + QK_HEAD, jnp.zeros((HEAD_PAD - QK_HEAD, tm), F32))


def _qkv(x1, seq, w, tables, tm):
    n = x1.shape[0]
    tiles_per_seq = seq // tm
    tok = lambda width: pl.BlockSpec((tm, width), lambda i: (i, 0))
    tab = pl.BlockSpec((tm, HEAD_PAD), lambda i: (i % tiles_per_seq, 0))
    tab_t = pl.BlockSpec((QK_ROPE // 2, tm), lambda i: (0, i % tiles_per_seq))
    consts = [w["g_mix"], w["w_a"], w["g_q"], w["w_uq_t"], w["g_kv"], w["w_uk"], w["w_uv_t"],
              w["g_qh_col"], w["g_kh"], w["g_kh_swap"]]
    return pl.pallas_call(
        _qkv_body,
        grid=(n // tm,),
        in_specs=([tok(D_MODEL)] + [_const_spec(a.shape) for a in consts]
                  + [tab, tab, tab_t, tab_t]),
        out_specs=[
            pl.BlockSpec((tm // Q_TILE, MLA_HEADS * HEAD_PAD, Q_TILE), lambda i: (i, 0, 0)),
            tok(MLA_HEADS * HEAD_PAD),
            pl.BlockSpec((tm // KV_CHUNK, MLA_HEADS * V_HEAD, KV_CHUNK), lambda i: (i, 0, 0)),
        ],
        out_shape=[jax.ShapeDtypeStruct((n // Q_TILE, MLA_HEADS * HEAD_PAD, Q_TILE), BF16),
                   jax.ShapeDtypeStruct((n, MLA_HEADS * HEAD_PAD), BF16),
                   jax.ShapeDtypeStruct((n // KV_CHUNK, MLA_HEADS * V_HEAD, KV_CHUNK), BF16)],
        compiler_params=_params(1),
        name="qkv",
    )(x1, *consts, *tables)


def _attn_flash_body(qt_ref, k_ref, vt_ref, o_ref, s_ref, acc_ref, ot_ref):
    nq, _, tq = qt_ref.shape
    n_kv = vt_ref.shape[0]
    ones = jnp.ones((SUM_ROWS, KV_CHUNK), BF16)
    m0 = jnp.full((1, tq), -jnp.inf, F32)
    heads = range(2)

    def scores(qi, j, slot):
        rows = pl.ds(pl.multiple_of(j * KV_CHUNK, KV_CHUNK), KV_CHUNK)
        cmax = []
        for hh in heads:
            st = _dot(k_ref[rows, hh * HEAD_PAD:(hh + 1) * HEAD_PAD],
                      qt_ref[qi, hh * HEAD_PAD:(hh + 1) * HEAD_PAD, :])
            s_ref[hh, slot] = st
            cmax.append(jnp.max(st, axis=0, keepdims=True))
        return tuple(cmax)

    def consume(j, slot, ms, cmax):
        out = []
        for hh in heads:
            m_new = jnp.maximum(ms[hh], cmax[hh])
            alpha = jnp.exp2(ms[hh] - m_new)
            pt = jnp.exp2(s_ref[hh, slot] - m_new).astype(BF16)
            vt = jnp.concatenate([vt_ref[j, hh * V_HEAD:(hh + 1) * V_HEAD, :], ones], axis=0)
            acc_ref[hh] = alpha * acc_ref[hh] + _dot(vt, pt)
            out.append(m_new)
        return tuple(out)

    def finish(qi):
        for hh in heads:
            ot_ref[qi, hh * V_HEAD:(hh + 1) * V_HEAD, :] = (
                acc_ref[hh, :V_HEAD, :] / acc_ref[hh, V_HEAD:V_HEAD + 1, :])
        acc_ref[...] = jnp.zeros_like(acc_ref)

    def q_tile(qi, carry):
        def group(g, carry):
            ms, c0, c1 = carry
            j = 4 * g
            c2 = scores(qi, j + 2, 2)
            ms = consume(j, 0, ms, c0)
            c3 = scores(qi, j + 3, 3)
            ms = consume(j + 1, 1, ms, c1)
            c0 = scores(qi, j + 4, 0)
            ms = consume(j + 2, 2, ms, c2)
            c1 = scores(qi, j + 5, 1)
            ms = consume(j + 3, 3, ms, c3)
            return ms, c0, c1

        ms, c0, c1 = lax.fori_loop(0, n_kv // 4 - 1, group, ((m0, m0),) + carry, unroll=True)
        q_next = jnp.minimum(qi + 1, nq - 1)
        c2 = scores(qi, n_kv - 2, 2)
        ms = consume(n_kv - 4, 0, ms, c0)
        c3 = scores(qi, n_kv - 1, 3)
        ms = consume(n_kv - 3, 1, ms, c1)
        c0 = scores(q_next, 0, 0)
        ms = consume(n_kv - 2, 2, ms, c2)
        c1 = scores(q_next, 1, 1)
        consume(n_kv - 1, 3, ms, c3)
        finish(qi)
        return c0, c1

    acc_ref[...] = jnp.zeros_like(acc_ref)
    lax.fori_loop(0, nq, q_tile, (scores(0, 0, 0), scores(0, 1, 1)), unroll=2)

    def emit(qi, carry):
        rows = pl.ds(pl.multiple_of(qi * tq, tq), tq)
        o_ref[rows, :] = ot_ref[qi].T.astype(BF16)
        return carry

    lax.fori_loop(0, nq, emit, 0)


def _attn_plain_body(qt_ref, k_ref, vt_ref, o_ref):
    nq, _, tq = qt_ref.shape
    n_kv = vt_ref.shape[0]
    ones = jnp.ones((SUM_ROWS, PLAIN_KEYS), BF16)

    def q_tile(qi, carry):
        n_steps = n_kv * KV_CHUNK // PLAIN_KEYS

        def scores(step, hh):
            rows = slice(step * PLAIN_KEYS, (step + 1) * PLAIN_KEYS)
            return _dot(k_ref[rows, hh * HEAD_PAD:(hh + 1) * HEAD_PAD],
                        qt_ref[qi, hh * HEAD_PAD:(hh + 1) * HEAD_PAD, :])

        acc = [None, None]
        st = [scores(0, 0), scores(0, 1)]
        for step in range(n_steps):
            st_next = [None, None]
            j, part = divmod(step * PLAIN_KEYS, KV_CHUNK)
            for hh in range(2):
                if step + 1 < n_steps:
                    st_next[hh] = scores(step + 1, hh)
                vt = jnp.concatenate(
                    [vt_ref[j, hh * V_HEAD:(hh + 1) * V_HEAD, part:part + PLAIN_KEYS], ones], axis=0)
                pv = _dot(vt, jnp.exp2(st[hh]).astype(BF16))
                acc[hh] = pv if acc[hh] is None else acc[hh] + pv
            st = st_next
        out = [a[:V_HEAD] / a[V_HEAD:V_HEAD + 1] for a in acc]
        rows = pl.ds(pl.multiple_of(qi * tq, tq), tq)
        o_ref[rows, :] = jnp.concatenate(out, axis=0).T.astype(BF16)
        return carry

    lax.fori_loop(0, nq, q_tile, 0, unroll=4)


def _attn(qt, k, vt, batch, seq, plain_ok):
    n = k.shape[0]
    nq = seq // Q_TILE
    n_kv = seq // KV_CHUNK
    acc = pltpu.VMEM((2, V_HEAD + SUM_ROWS, Q_TILE), F32)

    def call(body, scratch, name):
        return pl.pallas_call(
            body,
            grid=(batch, MLA_HEADS // 2),
            in_specs=[
                pl.BlockSpec((nq, 2 * HEAD_PAD, Q_TILE), lambda b, hp: (b, hp, 0)),
                pl.BlockSpec((seq, 2 * HEAD_PAD), lambda b, hp: (b, hp)),
                pl.BlockSpec((n_kv, 2 * V_HEAD, KV_CHUNK), lambda b, hp: (b, hp, 0)),
            ],
            out_specs=pl.BlockSpec((seq, 2 * V_HEAD), lambda b, hp: (b, hp)),
            out_shape=jax.ShapeDtypeStruct((n, MLA_HEADS * V_HEAD), BF16),
            scratch_shapes=scratch,
            compiler_params=_params(2),
            name=name,
        )

    flash = call(_attn_flash_body,
                 [pltpu.VMEM((2, 4, KV_CHUNK, Q_TILE), F32), acc,
                  pltpu.VMEM((nq, 2 * V_HEAD, Q_TILE), F32)], "attn_flash")
    plain = call(_attn_plain_body, [], "attn_plain")
    return lax.cond(plain_ok, plain, flash, qt, k, vt)


def _memkv_body(mem_ref, g_ref, w_ref, gk_ref, k_ref, v_ref):
    m = _rms(mem_ref[...], g_ref[...]).astype(BF16)
    kv = _dot(m, w_ref[...])
    width = XA_HEADS * XA_HEAD
    for hd in range(XA_HEADS):
        sl = slice(hd * XA_HEAD, (hd + 1) * XA_HEAD)
        k_ref[:, sl] = _rms(kv[:, sl], gk_ref[...]).astype(BF16)
    v_ref[...] = kv[:, width:].astype(BF16)


def _memkv(mem, w):
    n = mem.shape[0]
    width = XA_HEADS * XA_HEAD
    blk = lambda cols: pl.BlockSpec((N_MEM, cols), lambda b: (b, 0))
    consts = [w["g_mem"], w["w_mem_kv"], w["g_xk"]]
    return pl.pallas_call(
        _memkv_body,
        grid=(n // N_MEM,),
        in_specs=[blk(D_MODEL)] + [_const_spec(a.shape) for a in consts],
        out_specs=[blk(width), blk(width)],
        out_shape=[jax.ShapeDtypeStruct((n, width), BF16)] * 2,
        compiler_params=_params(1),
        name="memkv",
    )(mem, *consts)


def _mix_body(x_ref, xprev_ref, xnext_ref, o_ref, mk_ref, mv_ref, gmix_ref, win_ref,
              convw_ref, gxq_ref, womla_ref, woconv_ref, womem_ref,
              wout_ref, y_ref, u_ref, *, tiles_per_seq):
    i = pl.program_id(0)
    tm = x_ref.shape[0]
    x = x_ref[...]
    h = _rms(x, gmix_ref[...]).astype(BF16)

    o_cc, o_cx, o_xq = CONV_WIDTH, 2 * CONV_WIDTH, 3 * CONV_WIDTH
    o_gate = o_xq + XA_HEADS * XA_HEAD
    proj = lambda hh, start, width: _dot(hh, win_ref[:, start:start + width])

    def gated(hh):
        return proj(hh, o_cc, CONV_WIDTH) * proj(hh, o_cx, CONV_WIDTH)

    xq = proj(h, o_xq, XA_HEADS * XA_HEAD)
    first = (i % tiles_per_seq) == 0
    last = (i % tiles_per_seq) == tiles_per_seq - 1
    h_prev = _rms(xprev_ref[...], gmix_ref[...]).astype(BF16)
    h_next = _rms(xnext_ref[...], gmix_ref[...]).astype(BF16)
    u = gated(jnp.concatenate([h_prev, h, h_next], axis=0))
    u_ref[0:HALO, :] = jnp.where(first, 0.0, u[:HALO])
    u_ref[HALO:HALO + tm, :] = u[HALO:HALO + tm]
    u_ref[HALO + tm:, :] = jnp.where(last, 0.0, u[HALO + tm:])
    cb = proj(h, 0, CONV_WIDTH)

    def mem_scores(hd):
        sl = slice(hd * XA_HEAD, (hd + 1) * XA_HEAD)
        qh = (_rms(xq[:, sl], gxq_ref[...]) * (XA_HEAD ** -0.5)).astype(BF16)
        return _dot_nt(qh, mk_ref[:, sl])

    def mem_head(hd, s):
        sl = slice(hd * XA_HEAD, (hd + 1) * XA_HEAD)
        p = jnp.exp(s - jnp.max(s, axis=-1, keepdims=True))
        l = jnp.sum(p, axis=-1, keepdims=True)
        return (_dot(p.astype(BF16), mv_ref[:, sl]) / l).astype(BF16)

    gate = lambda b: jax.nn.sigmoid(proj(h, o_gate + b * D_MODEL, D_MODEL))
    s_mem = [mem_scores(hd) for hd in range(XA_HEADS)]
    merged = gate(0) * _dot(o_ref[...], womla_ref[...])
    heads = [mem_head(0, s_mem[0])]
    cw = convw_ref[...]
    conv = (u_ref[HALO - 1:HALO - 1 + tm, :] * cw[0:1, :] + u_ref[HALO:HALO + tm, :] * cw[1:2, :]
            + u_ref[HALO + 1:HALO + 1 + tm, :] * cw[2:3, :])
    merged = merged + gate(1) * _dot((cb * conv).astype(BF16), woconv_ref[...])
    heads += [mem_head(1, s_mem[1]), mem_head(2, s_mem[2])]
    gate_mem = gate(2)
    heads.append(mem_head(3, s_mem[3]))
    merged = merged + gate_mem * _dot(jnp.concatenate(heads, axis=-1), womem_ref[...])
    y_ref[...] = x + _dot(merged.astype(BF16), wout_ref[...])


def _mix(x1, o, mk, mv, seq, w, tm):
    n = x1.shape[0]
    tiles_per_seq = seq // tm
    halo = tm // HALO
    n_halo = n // HALO
    tok = lambda width: pl.BlockSpec((tm, width), lambda i: (i, 0))
    mem = pl.BlockSpec((N_MEM, XA_HEADS * XA_HEAD), lambda i: (i // tiles_per_seq, 0))
    consts = [w["g_mix"], w["w_in_mix"], w["conv_w"],
              w["g_xq"], w["w_o_mla"], w["w_o_conv"], w["w_o_mem"], w["w_out"]]
    return pl.pallas_call(
        functools.partial(_mix_body, tiles_per_seq=tiles_per_seq),
        grid=(n // tm,),
        in_specs=[
            tok(D_MODEL),
            pl.BlockSpec((HALO, D_MODEL), lambda i: (jnp.maximum(i * halo - 1, 0), 0)),
            pl.BlockSpec((HALO, D_MODEL), lambda i: (jnp.minimum((i + 1) * halo, n_halo - 1), 0)),
            tok(MLA_HEADS * V_HEAD), mem, mem,
        ] + [_const_spec(a.shape) for a in consts],
        out_specs=tok(D_MODEL),
        out_shape=jax.ShapeDtypeStruct((n, D_MODEL), F32),
        scratch_shapes=[pltpu.VMEM((tm + 2 * HALO, CONV_WIDTH), F32)],
        compiler_params=_params(1),
        name="mix",
    )(x1, x1, x1, o, mk, mv, *consts)


def _pad_heads(w, heads, width):
    k = w.shape[0]
    w = w.reshape(k, heads, width)
    return jnp.pad(w, ((0, 0), (0, 0), (0, HEAD_PAD - width))).reshape(k, heads * HEAD_PAD)


def _ffn_weights(norm, w_gu, w_down):
    return norm.reshape(1, D_MODEL), w_gu.astype(BF16), w_down.astype(BF16)


def _logit_bound(gain_q, gain_k):
    return 1.02 * (QK_HEAD ** 0.5) * LOG2_E * jnp.max(jnp.abs(gain_q)) * jnp.max(jnp.abs(gain_k))


def _rope_tables(seq):
    half = QK_ROPE // 2
    inv_freq = ROPE_BASE ** (-jnp.arange(half, dtype=F32) / half)
    ang = jnp.arange(seq, dtype=jnp.int32).astype(F32)[:, None] * inv_freq[None, :]
    cos, sin = jnp.cos(ang), jnp.sin(ang)
    zeros = lambda width: jnp.zeros((seq, width), F32)
    tail = HEAD_PAD - QK_HEAD
    cos_t = jnp.concatenate([jnp.ones((seq, QK_NOPE), F32), cos, cos, zeros(tail)], axis=1)
    sin_t = jnp.concatenate([zeros(QK_NOPE), -sin, sin, zeros(tail)], axis=1)
    return cos_t, sin_t, cos.T, sin.T


def _trunk(x, mem, w, ffn1, ffn2, tables):
    batch, seq, _ = x.shape
    x = x.reshape(batch * seq, D_MODEL)
    x1 = _ffn(x, *ffn1, tm=1024)
    qt, k, vt = _qkv(x1, seq, w, tables, tm=1024)
    o = _attn(qt, k, vt, batch, seq, w["plain_ok"])
    mk, mv = _memkv(mem.reshape(batch * N_MEM, D_MODEL), w)
    x2 = _mix(x1, o, mk, mv, seq, w, tm=512)
    y = _ffn(x2, *ffn2, tm=1024)
    return y.reshape(batch, seq, D_MODEL)


def kernel(x_prompt, x_sample, mem_prompt, mem_sample, ffn1_norm, ffn1_w_gu, ffn1_w_down, mix_norm, w_in, q_lora_norm, w_uq, kv_lora_norm, w_uk, w_uv, mla_q_norm, mla_k_norm, w_o_mla, conv_w, w_o_conv, mem_norm, w_mem_kv, xa_q_norm, xa_k_norm, w_o_mem, w_out, ffn2_norm, ffn2_w_gu, ffn2_w_down):
    ffn1 = _ffn_weights(ffn1_norm[0], ffn1_w_gu[0], ffn1_w_down[0])
    ffn2 = _ffn_weights(ffn2_norm[0], ffn2_w_gu[0], ffn2_w_down[0])

    w_in0 = w_in[0]
    o_cq, o_ckv, o_kr = 0, Q_LORA, Q_LORA + KV_LORA
    o_cb = o_kr + QK_ROPE
    cols = lambda start, width: w_in0[:, start:start + width]
    half = QK_ROPE // 2
    swap_halves = lambda a: jnp.concatenate([a[..., half:], a[..., :half]], axis=-1)
    rope_lanes = lambda a: jnp.pad(a, ((0, 0), (QK_NOPE, HEAD_PAD - QK_HEAD)))
    kr_pad = rope_lanes(cols(o_kr, QK_ROPE))
    kr_swap_pad = rope_lanes(swap_halves(cols(o_kr, QK_ROPE)))
    gk_swap = rope_lanes(swap_halves(mla_k_norm[0][QK_NOPE:]).reshape(1, QK_ROPE))
    row = lambda g: g.reshape(1, -1)
    pad_gain = lambda g: jnp.pad(g, (0, HEAD_PAD - QK_HEAD))
    w = {
        "g_mix": row(mix_norm[0]),
        "w_a": jnp.concatenate([cols(o_cq, Q_LORA), cols(o_ckv, KV_LORA), kr_pad, kr_swap_pad],
                               axis=1).astype(BF16),
        "g_q": row(q_lora_norm[0]),
        "w_uq_t": _pad_heads(w_uq[0], MLA_HEADS, QK_HEAD).T.astype(BF16),
        "g_kv": row(kv_lora_norm[0]),
        "w_uk": _pad_heads(w_uk[0], MLA_HEADS, QK_NOPE).astype(BF16),
        "w_uv_t": w_uv[0].T.astype(BF16),
        "g_qh_col": pad_gain(mla_q_norm[0]).reshape(HEAD_PAD, 1),
        "g_kh": pad_gain(mla_k_norm[0]).reshape(1, HEAD_PAD),
        "g_kh_swap": gk_swap,
        "w_in_mix": w_in0[:, o_cb:].astype(BF16),
        "conv_w": conv_w[0],
        "g_xq": row(xa_q_norm[0]),
        "w_o_mla": w_o_mla[0].astype(BF16),
        "w_o_conv": w_o_conv[0].astype(BF16),
        "w_o_mem": w_o_mem[0].astype(BF16),
        "w_out": w_out[0].astype(BF16),
        "g_mem": row(mem_norm[0]),
        "w_mem_kv": w_mem_kv[0].astype(BF16),
        "g_xk": row(xa_k_norm[0]),
        "plain_ok": _logit_bound(mla_q_norm[0], mla_k_norm[0]) <= MAX_PLAIN_LOGIT,
    }
    tables = _rope_tables(max(x_prompt.shape[1], x_sample.shape[1]))
    y_prompt = _trunk(x_prompt, mem_prompt, w, ffn1, ffn2, tables)
    y_sample = _trunk(x_sample, mem_sample, w, ffn1, ffn2, tables)
    return (y_prompt, y_sample)
```
